```python
import jax
import jax.numpy as jnp
from jax import lax
import numpy as np

D_MODEL = 1024
BATCH = 4
SEQ = 8192
DEPTH = 2

A_HEADS = 4
A_KDIM = 128
A_VDIM = 128
A_CHUNK = 64
B_GROUPS = 4
B_GDIM = 128
B_CHUNK = 128
C_QHEADS = 8
C_KVHEADS = 2
C_HDIM = 64
C_WINDOW = 128
C_BLOCK = 128
ROPE_THETA = 500000.0
ROPE_DIM = C_HDIM // 4
N_BRANCH = 3
D_FF = 4 * D_MODEL
EPS = 1e-6

A_KW = A_HEADS * A_KDIM
A_VW = A_HEADS * A_VDIM
B_W = B_GROUPS * B_GDIM
C_QW = C_QHEADS * C_HDIM
C_KVW = C_KVHEADS * C_HDIM
BRANCH_W = A_VW
SPLIT_SIZES = (A_KW, A_VW, A_KW, A_KW, A_VW, B_W, B_W, C_QW, C_KVW, C_KVW, N_BRANCH * D_MODEL)
D_IN = sum(SPLIT_SIZES)

kernel_name = 'hybrid_hgrn2_gmlp_swa_encoder'


def rms_norm(x, w):
    xf = x.astype(jnp.float32)
    y = xf * lax.rsqrt(jnp.mean(xf * xf, axis=-1, keepdims=True) + EPS)
    return (y * w.astype(jnp.float32)).astype(x.dtype)


def layer_norm(x):
    xf = x.astype(jnp.float32)
    xc = xf - jnp.mean(xf, axis=-1, keepdims=True)
    return (xc * lax.rsqrt(jnp.mean(xc * xc, axis=-1, keepdims=True) + EPS)).astype(x.dtype)


def partial_rope(t, pos):
    half = ROPE_DIM // 2
    inv = ROPE_THETA ** (-jnp.arange(half, dtype=jnp.float32) * (2.0 / ROPE_DIM))
    ang = pos[:, None] * inv[None, :]
    cos = jnp.cos(ang)[None, :, None, :]
    sin = jnp.sin(ang)[None, :, None, :]
    tf = t[..., :ROPE_DIM].astype(jnp.float32)
    t1, t2 = tf[..., :half], tf[..., half:]
    rot = jnp.concatenate([t1 * cos - t2 * sin, t2 * cos + t1 * sin], axis=-1)
    return jnp.concatenate([rot.astype(t.dtype), t[..., ROPE_DIM:]], axis=-1)


def window_attention(q, k, v, sink):
    bsz, s_len = q.shape[0], q.shape[1]
    nb = s_len // C_BLOCK
    grp = C_QHEADS // C_KVHEADS
    qb = q.reshape(bsz, nb, C_BLOCK, C_KVHEADS, grp, C_HDIM)
    pad = ((0, 0), (C_BLOCK, C_BLOCK), (0, 0), (0, 0))

    def band(t):
        tp = jnp.pad(t, pad).reshape(bsz, nb + 2, C_BLOCK, C_KVHEADS, C_HDIM)
        return jnp.concatenate([tp[:, :-2], tp[:, 1:-1], tp[:, 2:]], axis=2)

    kb, vb = band(k), band(v)
    scores = jnp.einsum('bnqhgd,bnkhd->bnhgqk', qb, kb).astype(jnp.float32) * (C_HDIM ** -0.5)
    blk = jnp.arange(nb)[:, None, None]
    qpos = blk * C_BLOCK + jnp.arange(C_BLOCK)[None, :, None]
    kpos = (blk - 1) * C_BLOCK + jnp.arange(3 * C_BLOCK)[None, None, :]
    mask = (jnp.abs(qpos - kpos) <= C_WINDOW) & (kpos >= 0) & (kpos < s_len)
    scores = jnp.where(mask[None, :, None, None], scores, -jnp.inf)
    sink_l = sink.astype(jnp.float32).reshape(C_KVHEADS, grp)[None, None, :, :, None, None]
    m = jnp.maximum(jnp.max(scores, axis=-1, keepdims=True), sink_l)
    p = jnp.exp(scores - m)
    p = p / (jnp.sum(p, axis=-1, keepdims=True) + jnp.exp(sink_l - m))
    o = jnp.einsum('bnhgqk,bnkhd->bnqhgd', p.astype(v.dtype), vb)
    return o.reshape(bsz, s_len, C_QW)


def hgrn2_gate(z, lb):
    zf = z.astype(jnp.float32)
    log_f = jnp.logaddexp(jnp.log(lb), jnp.log1p(-lb) + jax.nn.log_sigmoid(zf))
    key = (1.0 - lb) * jax.nn.sigmoid(-zf)
    return log_f, key


def gated_linear_scan(q, k, v, log_f):
    n, s_len = q.shape[0], q.shape[1]
    nc = s_len // A_CHUNK

    def to_chunks(t):
        return t.reshape(n, nc, A_CHUNK, A_HEADS, t.shape[-1]).transpose(1, 0, 3, 2, 4)

    qc, kc, vc = to_chunks(q), to_chunks(k), to_chunks(v)
    bc = jnp.cumsum(to_chunks(log_f), axis=3)
    tri = jnp.tril(jnp.ones((A_CHUNK, A_CHUNK), dtype=bool))

    def step(state, inp):
        qt, kt, vt, bt = inp
        b_last = bt[:, :, -1:, :]
        o_inter = jnp.einsum('nhck,nhkv->nhcv', qt * jnp.exp(bt), state)
        diff = bt[:, :, :, None, :] - bt[:, :, None, :, :]
        decay = jnp.exp(jnp.where(tri[:, :, None], diff, -jnp.inf))
        attn = jnp.einsum('nhtk,nhsk,nhtsk->nhts', qt, kt, decay)
        o_intra = jnp.einsum('nhts,nhsv->nhtv', attn, vt)
        state = jnp.exp(b_last[:, :, 0, :])[..., None] * state + jnp.einsum(
            'nhsk,nhsv->nhkv', kt * jnp.exp(b_last - bt), vt)
        return state, o_inter + o_intra

    state0 = jnp.zeros((n, A_HEADS, A_KDIM, A_VDIM), jnp.float32)
    _, o = lax.scan(step, state0, (qc, kc, vc, bc))
    return o.transpose(1, 0, 3, 2, 4).reshape(n, s_len, A_HEADS, A_VDIM)


def hgrn2_bidirectional(q, i, zf_fwd, zf_bwd, lb):
    bsz, s_len = q.shape[0], q.shape[1]
    shp = (bsz, s_len, A_HEADS, A_KDIM)
    qf = q.astype(jnp.float32).reshape(shp)
    vf = i.astype(jnp.float32).reshape(bsz, s_len, A_HEADS, A_VDIM)
    lf_f, k_f = hgrn2_gate(zf_fwd.reshape(shp), lb[0])
    lf_b, k_b = hgrn2_gate(zf_bwd.reshape(shp), lb[1])

    def flip(t):
        return t[:, ::-1]

    o = gated_linear_scan(
        jnp.concatenate([qf, flip(qf)], axis=0),
        jnp.concatenate([k_f, flip(k_b)], axis=0),
        jnp.concatenate([vf, flip(vf)], axis=0),
        jnp.concatenate([lf_f, flip(lf_b)], axis=0))
    return o[:bsz] + flip(o[bsz:])


def spatial_gating(u, v, w_s, b_s):
    bsz, s_len = u.shape[0], u.shape[1]
    nb = s_len // B_CHUNK
    vn = layer_norm(v).reshape(bsz, nb, B_CHUNK, B_GROUPS, B_GDIM)
    mixed = jnp.einsum('bnpgc,gqp->bnqgc', vn, w_s) + b_s.T[None, None, :, :, None]
    return u * mixed.reshape(bsz, s_len, B_W)


def setup_inputs(seed: int = 0) -> dict:
    key = jax.random.key(seed)
    ks = jax.random.split(key, 14)
    f32 = jnp.float32

    def nrm(k, shape, scale):
        return jax.random.normal(k, shape, f32) * scale

    return {
        'x': nrm(ks[0], (BATCH, SEQ, D_MODEL), 1.0),
        'w_in': nrm(ks[1], (DEPTH, D_MODEL, D_IN), D_MODEL ** -0.5),
        'ln1': 1.0 + nrm(ks[2], (DEPTH, D_MODEL), 0.02),
        'lb_logits': nrm(ks[3], (DEPTH, 2, A_KW), 0.5),
        'a_norm': 1.0 + nrm(ks[4], (DEPTH, A_VDIM), 0.02),
        'w_s': nrm(ks[5], (DEPTH, B_GROUPS, B_CHUNK, B_CHUNK), B_CHUNK ** -0.5),
        'b_s': 1.0 + nrm(ks[6], (DEPTH, B_GROUPS, B_CHUNK), 0.1),
        'sink': nrm(ks[7], (DEPTH, C_QHEADS), 0.5),
        'w_br': nrm(ks[8], (DEPTH, N_BRANCH, BRANCH_W, D_MODEL), BRANCH_W ** -0.5),
        'w_out': nrm(ks[9], (DEPTH, D_MODEL, D_MODEL), D_MODEL ** -0.5),
        'ln2': 1.0 + nrm(ks[10], (DEPTH, D_MODEL), 0.02),
        'w_up': nrm(ks[11], (DEPTH, D_MODEL, D_FF), D_MODEL ** -0.5),
        'w_down': nrm(ks[12], (DEPTH, D_FF, D_MODEL), D_FF ** -0.5),
        'final_norm': 1.0 + nrm(ks[13], (D_MODEL,), 0.02),
    }


def reference(x, w_in, ln1, lb_logits, a_norm, w_s, b_s, sink, w_br, w_out, ln2, w_up, w_down, final_norm):
    bsz, s_len = x.shape[0], x.shape[1]
    pos = jnp.arange(s_len, dtype=jnp.float32)
    p = jax.nn.softmax(lb_logits.astype(jnp.float32), axis=0)
    cum = jnp.cumsum(p, axis=0)
    lower = (cum - cum[0:1]).reshape(DEPTH, 2, A_HEADS, A_KDIM)
    offsets = np.cumsum(SPLIT_SIZES)[:-1].tolist()
    for l in range(DEPTH):
        xn = rms_norm(x, ln1[l])
        z = xn @ w_in[l]
        a_q, a_i, a_ff, a_fb, a_g, b_u, b_v, c_q, c_k, c_v, gz = jnp.split(z, offsets, axis=-1)
        o_a = hgrn2_bidirectional(a_q, a_i, a_ff, a_fb, lower[l])
        y_a = (rms_norm(o_a, a_norm[l]) * jax.nn.silu(a_g.astype(jnp.float32).reshape(o_a.shape)))
        y_a = y_a.astype(x.dtype).reshape(bsz, s_len, A_VW)
        y_b = spatial_gating(jax.nn.gelu(b_u), jax.nn.gelu(b_v), w_s[l], b_s[l])
        qh = partial_rope(c_q.reshape(bsz, s_len, C_QHEADS, C_HDIM), pos)
        kh = partial_rope(c_k.reshape(bsz, s_len, C_KVHEADS, C_HDIM), pos)
        vh = c_v.reshape(bsz, s_len, C_KVHEADS, C_HDIM)
        y_c = window_attention(qh, kh, vh, sink[l])
        gates = jax.nn.sigmoid(gz.reshape(bsz, s_len, N_BRANCH, D_MODEL))
        merged = (gates[:, :, 0] * (y_a @ w_br[l, 0])
                  + gates[:, :, 1] * (y_b @ w_br[l, 1])
                  + gates[:, :, 2] * (y_c @ w_br[l, 2]))
        x = x + merged @ w_out[l]
        hn = rms_norm(x, ln2[l])
        x = x + jnp.square(jax.nn.relu(hn @ w_up[l])) @ w_down[l]
    return rms_norm(x, final_norm)
```

```python
import functools
import math

import jax
import jax.numpy as jnp
from jax import lax
from jax.experimental import pallas as pl
from jax.experimental.pallas import tpu as pltpu

F32 = jnp.float32
BF16 = jnp.bfloat16

D_MODEL = 1024
A_HEADS = 4
A_DIM = 128
B_GROUPS = 4
B_CHUNK = 128
C_QHEADS = 8
C_HDIM = 64
C_BLOCK = 128
ROPE_THETA = 500000.0
ROPE_DIM = C_HDIM // 4
N_BRANCH = 3
D_FF = 4 * D_MODEL
EPS = 1e-6
BRANCH_W = 512

OFF_AQ, OFF_AI, OFF_AFF, OFF_AFB, OFF_AG = 0, 512, 1024, 1536, 2048
OFF_BU, OFF_BV, OFF_CQ, OFF_GZ, OFF_KV = 2560, 3072, 3584, 4096, 7168
D_IN = 7424
ORIG_KV, ORIG_GZ = 4096, 4352

LANES = 128
SCAN_CHUNK = 128
NEG_BIG = -1e30
VMEM_LIMIT = 56 * 1024 * 1024


def _sigmoid(x):
    return 1.0 / (1.0 + jnp.exp(-x))


def _rms(x, g):
    return x * lax.rsqrt(jnp.mean(x * x, axis=-1, keepdims=True) + EPS) * g


def _dot(a, b):
    return jnp.dot(a, b, preferred_element_type=F32)


def _dot_nt(a, b):
    return lax.dot_general(a, b, (((1,), (1,)), ((), ())), preferred_element_type=F32)


def _dot_tn(a, b):
    return lax.dot_general(a, b, (((0,), (0,)), ((), ())), preferred_element_type=F32)


def _inproj_kernel(x_ref, g_ref, w_ref, z_ref):
    xn = _rms(x_ref[...], g_ref[...]).astype(BF16)
    z_ref[...] = _dot(xn, w_ref[...])


def _inproj(x, g, w, tm):
    m = x.shape[0]
    tn = D_IN // 2
    return pl.pallas_call(
        _inproj_kernel,
        out_shape=jax.ShapeDtypeStruct((m, D_IN), F32),
        grid=(D_IN // tn, m // tm),
        in_specs=[
            pl.BlockSpec((tm, D_MODEL), lambda j, i: (i, 0)),
            pl.BlockSpec((1, D_MODEL), lambda j, i: (0, 0)),
            pl.BlockSpec((D_MODEL, tn), lambda j, i: (0, j)),
        ],
        out_specs=pl.BlockSpec((tm, tn), lambda j, i: (i, j)),
        compiler_params=pltpu.CompilerParams(
            dimension_semantics=("arbitrary", "arbitrary"), vmem_limit_bytes=VMEM_LIMIT),
        name="inproj",
    )(x, g, w)


def _split3(x):
    hi = x.astype(BF16)
    r1 = x - hi.astype(F32)
    mid = r1.astype(BF16)
    lo = (r1 - mid.astype(F32)).astype(BF16)
    return hi, mid, lo


def _seg_bcast(b, m, pick):
    c = b.shape[0]
    nb = c // (2 * m)
    r = b.reshape(nb, 2 * m, LANES)[:, pick:pick + 1, :]
    return jnp.broadcast_to(r, (nb, 2 * m, LANES)).reshape(c, LANES)


def _scan_stream(q, v, z, log_lb, log1m_lb, one_m_lb, st, fwd, row, tri, rt, rs):
    c = q.shape[0]
    e = jnp.exp(-jnp.abs(z))
    log_sig = jnp.minimum(z, 0.0) - jnp.log1p(e)
    y = log1m_lb + log_sig
    lf = jnp.maximum(log_lb, y) + jnp.log1p(jnp.exp(-jnp.abs(log_lb - y)))
    k = one_m_lb * jnp.where(z >= 0, e, 1.0) / (1.0 + e)

    hi, mid, lo = _split3(lf)
    b = _dot(tri, hi) + _dot(tri, mid) + _dot(tri, lo)
    b_tot = b[c - 1:c, :] if fwd else b[0:1, :]

    qb = q.astype(BF16)
    kb = k.astype(BF16)
    attn = jnp.where(rt == rs, _dot_nt(qb, kb), 0.0)

    up1 = pltpu.roll(lf, c - 1, axis=0)
    dn1 = pltpu.roll(lf, 1, axis=0)
    m = c // 2
    while m >= 1:
        second = (row & m) != 0
        t_row = second if fwd else jnp.logical_not(second)
        if m >= 4:
            ref = _seg_bcast(b, m, m - 1 if fwd else m)
            d = b - ref
            ex = jnp.where(t_row, d, -d)
        elif m == 2:
            r4 = row & 3
            if fwd:
                ex = jnp.where(r4 == 0, up1, jnp.where(r4 == 1, 0.0, jnp.where(r4 == 2, lf, lf + dn1)))
            else:
                ex = jnp.where(r4 == 0, lf + up1, jnp.where(r4 == 1, lf, jnp.where(r4 == 2, 0.0, dn1)))
        else:
            ex = jnp.where(t_row, lf, 0.0)
        w = jnp.exp(ex)
        qs = jnp.where(t_row, q * w, 0.0).astype(BF16)
        ks = jnp.where(t_row, 0.0, k * w).astype(BF16)
        a = _dot_nt(qs, ks)
        if 2 * m < c:
            sh = int(math.log2(2 * m))
            a = jnp.where((rt >> sh) == (rs >> sh), a, 0.0)
        attn = attn + a
        m //= 2

    o = _dot(attn.astype(BF16), v.astype(BF16)) + _dot_nt((q * jnp.exp(b)).astype(BF16), st.astype(BF16))
    k_st = (k * jnp.exp(b_tot - b)).astype(BF16)
    st_new = st * jnp.exp(b_tot) + _dot_tn(v.astype(BF16), k_st)
    return o, st_new


def _scan_kernel(qf_ref, vf_ref, zf_ref, qb_ref, vb_ref, zb_ref, gc_ref, of_ref, ob_ref, st_ref):
    @pl.when(pl.program_id(1) == 0)
    def _():
        st_ref[...] = jnp.zeros_like(st_ref)

    c = qf_ref.shape[0]
    row = lax.broadcasted_iota(jnp.int32, (c, LANES), 0)
    rt = lax.broadcasted_iota(jnp.int32, (c, c), 0)
    rs = lax.broadcasted_iota(jnp.int32, (c, c), 1)
    streams = ((True, qf_ref, vf_ref, zf_ref, of_ref), (False, qb_ref, vb_ref, zb_ref, ob_ref))
    for d, (fwd, q_ref, v_ref, z_ref, o_ref) in enumerate(streams):
        tri = jnp.where((rs <= rt) if fwd else (rs >= rt), 1.0, 0.0).astype(BF16)
        for h in range(A_HEADS):
            sl = slice(h * A_DIM, (h + 1) * A_DIM)
            o, st = _scan_stream(
                q_ref[:, sl], v_ref[:, sl], z_ref[:, sl],
                gc_ref[d, 0:1, sl], gc_ref[d, 1:2, sl], gc_ref[d, 2:3, sl],
                st_ref[d, h], fwd, row, tri, rt, rs)
            o_ref[:, sl] = o
            st_ref[d, h] = st


def _scan(z, gc, bsz, seq):
    m = z.shape[0]
    c = SCAN_CHUNK
    nc = seq // c
    w = A_HEADS * A_DIM

    def fspec(col):
        return pl.BlockSpec((c, w), lambda b, i: (b * nc + i, col))

    def bspec(col):
        return pl.BlockSpec((c, w), lambda b, i: (b * nc + nc - 1 - i, col))

    return pl.pallas_call(
        _scan_kernel,
        out_shape=(jax.ShapeDtypeStruct((m, w), F32), jax.ShapeDtypeStruct((m, w), F32)),
        grid=(bsz, nc),
        in_specs=[
            fspec(OFF_AQ // w), fspec(OFF_AI // w), fspec(OFF_AFF // w),
            bspec(OFF_AQ // w), bspec(OFF_AI // w), bspec(OFF_AFB // w),
            pl.BlockSpec((2, 8, w), lambda b, i: (0, 0, 0)),
        ],
        out_specs=(fspec(0), bspec(0)),
        scratch_shapes=[pltpu.VMEM((2, A_HEADS, A_DIM, A_DIM), F32)],
        compiler_params=pltpu.CompilerParams(
            dimension_semantics=("arbitrary", "arbitrary"), vmem_limit_bytes=VMEM_LIMIT),
        name="hgrn2_scan",
    )(z, z, z, z, z, z, gc)


def _gelu_tanh(x):
    return 0.5 * x * (1.0 + jnp.tanh(math.sqrt(2.0 / math.pi) * (x + 0.044715 * (x * x * x))))


def _rope(t, tab):
    return (t * tab[:, 0:LANES]
            + pltpu.roll(t, LANES - ROPE_DIM // 2, axis=1) * tab[:, LANES:2 * LANES]
            + pltpu.roll(t, ROPE_DIM // 2, axis=1) * tab[:, 2 * LANES:3 * LANES])


def _branch_kernel(sink_ref, ag_ref, of_ref, ob_ref, bu_ref, bv_ref, cq_ref, kv_ref, kvp_ref, kvn_ref,
                   rp_ref, rpp_ref, rpn_ref, an_ref, ws_ref, bs_ref, y_ref, *, tiles_per_seq):
    tm = ag_ref.shape[0]
    nsub = tm // C_BLOCK
    pid = pl.program_id(0)

    for h in range(A_HEADS):
        sl = slice(h * A_DIM, (h + 1) * A_DIM)
        g = ag_ref[:, sl]
        y_ref[:, sl] = (_rms(of_ref[:, sl] + ob_ref[:, sl], an_ref[...]) * (g * _sigmoid(g))).astype(BF16)

    for j in range(nsub):
        rows = slice(j * B_CHUNK, (j + 1) * B_CHUNK)
        u = _gelu_tanh(bu_ref[rows, :])
        v = _gelu_tanh(bv_ref[rows, :])
        vc = v - jnp.mean(v, axis=-1, keepdims=True)
        vn = (vc * lax.rsqrt(jnp.mean(vc * vc, axis=-1, keepdims=True) + EPS)).astype(BF16)
        for g in range(B_GROUPS):
            sl = slice(g * LANES, (g + 1) * LANES)
            mixed = _dot(ws_ref[g], vn[:, sl]) + bs_ref[:, sl]
            y_ref[rows, BRANCH_W + g * LANES:BRANCH_W + (g + 1) * LANES] = (u[:, sl] * mixed).astype(BF16)

    k_all = jnp.concatenate([_rope(kvp_ref[:, 0:LANES], rpp_ref[...]),
                             _rope(kv_ref[:, 0:LANES], rp_ref[...]),
                             _rope(kvn_ref[:, 0:LANES], rpn_ref[...])], axis=0)
    v_all = jnp.concatenate([kvp_ref[:, LANES:], kv_ref[:, LANES:], kvn_ref[:, LANES:]], axis=0)
    lo = lax.broadcasted_iota(jnp.int32, (1, LANES), 1) < C_HDIM

    def placed(t):
        sw = pltpu.roll(t, C_HDIM, axis=1)
        lo0 = jnp.where(lo, t, 0.0).astype(BF16)
        hi0 = jnp.where(lo, 0.0, sw).astype(BF16)
        lo1 = jnp.where(lo, sw, 0.0).astype(BF16)
        hi1 = jnp.where(lo, 0.0, t).astype(BF16)
        return ((lo0, hi0), (lo1, hi1))

    k_ops = placed(k_all)
    v_ops = placed(v_all)

    ii = lax.broadcasted_iota(jnp.int32, (C_BLOCK, 3 * C_BLOCK), 0)
    jj = lax.broadcasted_iota(jnp.int32, (C_BLOCK, 3 * C_BLOCK), 1)
    band = (jj >= ii) & (jj <= ii + 2 * C_BLOCK)
    seq_tile = pid % tiles_per_seq
    has_prev = seq_tile != 0
    has_next = seq_tile != tiles_per_seq - 1

    n_pairs = C_QHEADS // 2
    for p in range(n_pairs):
        g = p // (n_pairs // 2)
        q_pair = _rope(cq_ref[:, p * LANES:(p + 1) * LANES], rp_ref[...]).astype(BF16)
        for j in range(nsub):
            rows = slice(j * C_BLOCK, (j + 1) * C_BLOCK)
            keys = slice(j * C_BLOCK, (j + 3) * C_BLOCK)
            mask = band
            if j == 0:
                mask = mask & ((jj >= C_BLOCK) | has_prev)
            if j == nsub - 1:
                mask = mask & ((jj < 2 * C_BLOCK) | has_next)
            acc = None
            for half in range(2):
                sink = sink_ref[2 * p + half]
                s = jnp.where(mask, _dot_nt(q_pair[rows, :], k_ops[g][half][keys, :]), NEG_BIG)
                mx = jnp.maximum(jnp.max(s, axis=-1, keepdims=True), sink)
                pr = jnp.exp(s - mx)
                den = jnp.sum(pr, axis=-1, keepdims=True) + jnp.exp(sink - mx)
                o = _dot(pr.astype(BF16), v_ops[g][half][keys, :]) * (1.0 / den)
                acc = o if acc is None else acc + o
            y_ref[rows, 2 * BRANCH_W + p * LANES:2 * BRANCH_W + (p + 1) * LANES] = acc.astype(BF16)


def _branches(z, o_f, o_b, rope_tab, sink, a_norm, ws, bs_tile, seq, tm):
    m = z.shape[0]
    w = BRANCH_W
    nblk = m // C_BLOCK
    sub = tm // C_BLOCK
    tiles_per_seq = seq // tm
    seq_blocks = seq // C_BLOCK
    kvw = 2 * LANES

    def zspec(off):
        return pl.BlockSpec((tm, w), lambda i: (i, off // w))

    row_spec = pl.BlockSpec((tm, w), lambda i: (i, 0))
    return pl.pallas_call(
        functools.partial(_branch_kernel, tiles_per_seq=tiles_per_seq),
        out_shape=jax.ShapeDtypeStruct((m, N_BRANCH * w), BF16),
        grid=(m // tm,),
        in_specs=[
            pl.BlockSpec(memory_space=pltpu.SMEM),
            zspec(OFF_AG), row_spec, row_spec, zspec(OFF_BU), zspec(OFF_BV), zspec(OFF_CQ),
            pl.BlockSpec((tm, kvw), lambda i: (i, OFF_KV // kvw)),
            pl.BlockSpec((C_BLOCK, kvw), lambda i: (jnp.maximum(i * sub - 1, 0), OFF_KV // kvw)),
            pl.BlockSpec((C_BLOCK, kvw), lambda i: (jnp.minimum((i + 1) * sub, nblk - 1), OFF_KV // kvw)),
            pl.BlockSpec((tm, 3 * LANES), lambda i: (i % tiles_per_seq, 0)),
            pl.BlockSpec((C_BLOCK, 3 * LANES), lambda i: ((i * sub + seq_blocks - 1) % seq_blocks, 0)),
            pl.BlockSpec((C_BLOCK, 3 * LANES), lambda i: (((i + 1) * sub) % seq_blocks, 0)),
            pl.BlockSpec((1, A_DIM), lambda i: (0, 0)),
            pl.BlockSpec((B_GROUPS, B_CHUNK, B_CHUNK), lambda i: (0, 0, 0)),
            pl.BlockSpec((B_CHUNK, w), lambda i: (0, 0)),
        ],
        out_specs=pl.BlockSpec((tm, N_BRANCH * w), lambda i: (i, 0)),
        compiler_params=pltpu.CompilerParams(
            dimension_semantics=("arbitrary",), vmem_limit_bytes=VMEM_LIMIT),
        name="branches",
    )(sink, z, o_f, o_b, z, z, z, z, z, z, rope_tab, rope_tab, rope_tab, a_norm, ws, bs_tile)


def _merge_kernel(y_ref, g0_ref, g1_ref, g2_ref, x_ref, wbr_ref, wout_ref, o_ref):
    acc = None
    for i, g_ref in enumerate((g0_ref, g1_ref, g2_ref)):
        t = _sigmoid(g_ref[...]) * _dot(y_ref[:, i * BRANCH_W:(i + 1) * BRANCH_W], wbr_ref[i])
        acc = t if acc is None else acc + t
    o_ref[...] = x_ref[...] + _dot(acc.astype(BF16), wout_ref[...])


def _merge(y, z, x, wbr, wout, tm):
    m = x.shape[0]

    def gspec(i):
        return pl.BlockSpec((tm, D_MODEL), lambda r: (r, OFF_GZ // D_MODEL + i))

    return pl.pallas_call(
        _merge_kernel,
        out_shape=jax.ShapeDtypeStruct((m, D_MODEL), F32),
        grid=(m // tm,),
        in_specs=[
            pl.BlockSpec((tm, N_BRANCH * BRANCH_W), lambda r: (r, 0)),
            gspec(0), gspec(1), gspec(2),
            pl.BlockSpec((tm, D_MODEL), lambda r: (r, 0)),
            pl.BlockSpec((N_BRANCH, BRANCH_W, D_MODEL), lambda r: (0, 0, 0)),
            pl.BlockSpec((D_MODEL, D_MODEL), lambda r: (0, 0)),
        ],
        out_specs=pl.BlockSpec((tm, D_MODEL), lambda r: (r, 0)),
        compiler_params=pltpu.CompilerParams(
            dimension_semantics=("arbitrary",), vmem_limit_bytes=VMEM_LIMIT),
        name="merge",
    )(y, z, z, z, x, wbr, wout)


def _mlp_kernel(x_ref, g_ref, wu_ref, wd_ref, fn_ref, o_ref, *, final):
    x = x_ref[...]
    hn = _rms(x, g_ref[...]).astype(BF16)
    h = jnp.maximum(_dot(hn, wu_ref[...]), 0.0)
    x = x + _dot((h * h).astype(BF16), wd_ref[...])
    if final:
        x = _rms(x, fn_ref[...])
    o_ref[...] = x


def _mlp(x, g, wu, wd, fn, final, tm):
    m = x.shape[0]
    const = lambda r: (0, 0)
    return pl.pallas_call(
        functools.partial(_mlp_kernel, final=final),
        out_shape=jax.ShapeDtypeStruct((m, D_MODEL), F32),
        grid=(m // tm,),
        in_specs=[
            pl.BlockSpec((tm, D_MODEL), lambda r: (r, 0)),
            pl.BlockSpec((1, D_MODEL), const),
            pl.BlockSpec((D_MODEL, D_FF), const, pipeline_mode=pl.Buffered(1)),
            pl.BlockSpec((D_FF, D_MODEL), const, pipeline_mode=pl.Buffered(1)),
            pl.BlockSpec((1, D_MODEL), const),
        ],
        out_specs=pl.BlockSpec((tm, D_MODEL), lambda r: (r, 0)),
        compiler_params=pltpu.CompilerParams(
            dimension_semantics=("arbitrary",), vmem_limit_bytes=VMEM_LIMIT),
        name="mlp",
    )(x, g, wu, wd, fn)


def _rope_table(seq):
    half = ROPE_DIM // 2
    inv = ROPE_THETA ** (-jnp.arange(half, dtype=F32) * (2.0 / ROPE_DIM))
    ang = jnp.arange(seq, dtype=F32)[:, None] * inv[None, :]
    cos, sin = jnp.cos(ang), jnp.sin(ang)
    pad = jnp.zeros((seq, C_HDIM - ROPE_DIM), F32)
    zero = jnp.zeros_like(sin)
    cos_h = jnp.concatenate([cos, cos, pad + 1.0], axis=1)
    sin_a = jnp.concatenate([-sin, zero, pad], axis=1)
    sin_b = jnp.concatenate([zero, sin, pad], axis=1)
    reps = LANES // C_HDIM
    return jnp.concatenate([jnp.tile(t, (1, reps)) for t in (cos_h, sin_a, sin_b)], axis=1)


def kernel(x, w_in, ln1, lb_logits, a_norm, w_s, b_s, sink, w_br, w_out, ln2, w_up, w_down, final_norm):
    bsz, seq, _ = x.shape
    depth = w_in.shape[0]
    m = bsz * seq
    tm = min(512, seq)

    p = jax.nn.softmax(lb_logits.astype(F32), axis=0)
    cum = jnp.cumsum(p, axis=0)
    lower = (cum - cum[0:1]).reshape(depth, 2, 1, A_HEADS * A_DIM)
    gate_consts = jnp.concatenate(
        [jnp.log(lower), jnp.log1p(-lower), 1.0 - lower, jnp.zeros((depth, 2, 5, A_HEADS * A_DIM), F32)], axis=2)

    rope_tab = _rope_table(seq)
    q_scale = jnp.ones((D_IN,), F32).at[OFF_CQ:OFF_CQ + BRANCH_W].set(C_HDIM ** -0.5)

    h = x.reshape(m, D_MODEL)
    for l in range(depth):
        w = w_in[l]
        w_perm = (jnp.concatenate([w[:, :ORIG_KV], w[:, ORIG_GZ:], w[:, ORIG_KV:ORIG_GZ]], axis=1)
                  * q_scale).astype(BF16)
        bs_tile = jnp.repeat(b_s[l].T, B_CHUNK, axis=1)
        z = _inproj(h, ln1[l][None, :], w_perm, tm)
        o_f, o_b = _scan(z, gate_consts[l], bsz, seq)
        y = _branches(z, o_f, o_b, rope_tab, sink[l], a_norm[l][None, :], w_s[l].astype(BF16), bs_tile, seq, tm)
        h = _merge(y, z, h, w_br[l].astype(BF16), w_out[l].astype(BF16), tm)
        h = _mlp(h, ln2[l][None, :], w_up[l].astype(BF16), w_down[l].astype(BF16),
                 final_norm[None, :], l == depth - 1, tm)
    return h.reshape(bsz, seq, D_MODEL)
```

```python
import functools
import math

import numpy as np
import jax
import jax.numpy as jnp
from jax import lax
from jax.experimental import pallas as pl
from jax.experimental.pallas import tpu as pltpu

F32 = jnp.float32
BF16 = jnp.bfloat16

D_MODEL = 1024
A_HEADS = 4
A_DIM = 128
A_W = A_HEADS * A_DIM
B_GROUPS = 4
B_CHUNK = 128
C_QHEADS = 8
C_HDIM = 64
C_BLOCK = 128
ROPE_THETA = 500000.0
ROPE_DIM = C_HDIM // 4
N_BRANCH = 3
D_FF = 4 * D_MODEL
EPS = 1e-6
BRANCH_W = 512

OFF_AQ, OFF_AI, OFF_AFF, OFF_AFB, OFF_AG = 0, 512, 1024, 1536, 2048
OFF_BU, OFF_BV, OFF_CQ, OFF_GZ, OFF_KV = 2560, 3072, 3584, 4096, 7168
D_IN = 7424
ORIG_KV, ORIG_GZ = 4096, 4352

LANES = 128
SCAN_CHUNK = 128
KV_W = 8 * LANES
NEG_BIG = -1e30
LOG2E = math.log2(math.e)
VMEM_LIMIT = 56 * 1024 * 1024


def _sigmoid(x):
    return 1.0 / (1.0 + jnp.exp(-x))


def _rms(x, g):
    return x * lax.rsqrt(jnp.mean(x * x, axis=-1, keepdims=True) + EPS) * g


def _dot(a, b):
    return jnp.dot(a, b, preferred_element_type=F32)


def _dot_nt(a, b):
    return lax.dot_general(a, b, (((1,), (1,)), ((), ())), preferred_element_type=F32)


def _dot_tn(a, b):
    return lax.dot_general(a, b, (((0,), (0,)), ((), ())), preferred_element_type=F32)


def _gelu_tanh(x):
    return 0.5 * x * (1.0 + jnp.tanh(math.sqrt(2.0 / math.pi) * (x + 0.044715 * (x * x * x))))


def _rope(t, tab):
    return (t * tab[:, 0:LANES]
            + pltpu.roll(t, LANES - ROPE_DIM // 2, axis=1) * tab[:, LANES:2 * LANES]
            + pltpu.roll(t, ROPE_DIM // 2, axis=1) * tab[:, 2 * LANES:3 * LANES])


def _inproj_kernel(x_ref, g_ref, w_ref, gc_ref, rp_ref, ws_ref, bs_ref,
                   qv_ref, lf_ref, kk_ref, sg_ref, yb_ref, cq_ref, kv_ref, gt_ref):
    tm = x_ref.shape[0]
    xn = _rms(x_ref[...], g_ref[...]).astype(BF16)

    def proj(off, width):
        return _dot(xn, w_ref[:, off:off + width])

    qv_ref[...] = proj(OFF_AQ, 2 * A_W).astype(BF16)

    for d in range(2):
        z = proj(OFF_AFF + d * A_W, A_W)
        e = jnp.exp(-jnp.abs(z))
        r = 1.0 / (1.0 + e)
        log_sig = jnp.minimum(z, 0.0) + jnp.log(r)
        y = gc_ref[d, 1:2, :] + log_sig
        a = gc_ref[d, 0:1, :]
        lf = jnp.maximum(a, y) + jnp.log(1.0 + jnp.exp(-jnp.abs(a - y)))
        lf_ref[:, d * A_W:(d + 1) * A_W] = lf * LOG2E
        kk_ref[:, d * A_W:(d + 1) * A_W] = (gc_ref[d, 2:3, :] * jnp.where(z >= 0, e, 1.0) * r).astype(BF16)

    g = proj(OFF_AG, A_W)
    sg_ref[...] = (g * _sigmoid(g)).astype(BF16)

    u = _gelu_tanh(proj(OFF_BU, BRANCH_W))
    v = _gelu_tanh(proj(OFF_BV, BRANCH_W))
    vc = v - jnp.mean(v, axis=-1, keepdims=True)
    vn = (vc * lax.rsqrt(jnp.mean(vc * vc, axis=-1, keepdims=True) + EPS)).astype(BF16)
    for j in range(tm // B_CHUNK):
        rows = slice(j * B_CHUNK, (j + 1) * B_CHUNK)
        for grp in range(B_GROUPS):
            sl = slice(grp * LANES, (grp + 1) * LANES)
            mixed = _dot(ws_ref[grp], vn[rows, sl]) + bs_ref[:, sl]
            yb_ref[rows, sl] = (u[rows, sl] * mixed).astype(BF16)

    tab = rp_ref[...]
    zq = proj(OFF_CQ, BRANCH_W)
    for p in range(BRANCH_W // LANES):
        sl = slice(p * LANES, (p + 1) * LANES)
        cq_ref[:, sl] = (_rope(zq[:, sl], tab) * LOG2E).astype(BF16)
    zkv = proj(OFF_KV, 2 * LANES)
    lo = lax.broadcasted_iota(jnp.int32, (1, LANES), 1) < C_HDIM
    for i, (t, fill) in enumerate(((_rope(zkv[:, 0:LANES], tab), 0.0), (zkv[:, LANES:], 1.0))):
        sw = pltpu.roll(t, C_HDIM, axis=1)
        placed = (jnp.where(lo, t, fill), jnp.where(lo, fill, sw), jnp.where(lo, sw, fill), jnp.where(lo, fill, t))
        for c, val in enumerate(placed):
            kv_ref[:, (4 * i + c) * LANES:(4 * i + c + 1) * LANES] = val.astype(BF16)

    for i in range(N_BRANCH):
        gt_ref[:, i * D_MODEL:(i + 1) * D_MODEL] = _sigmoid(proj(OFF_GZ + i * D_MODEL, D_MODEL)).astype(BF16)


def _inproj(x, g, w, gc, rope_tab, ws, bs_tile, seq, tm):
    m = x.shape[0]
    tiles_per_seq = seq // tm
    row = lambda i: (i, 0)
    c2 = lambda i: (0, 0)
    c3 = lambda i: (0, 0, 0)
    widths = (2 * A_W, 2 * A_W, 2 * A_W, A_W, BRANCH_W, BRANCH_W, KV_W, N_BRANCH * D_MODEL)
    dtypes = (BF16, F32, BF16, BF16, BF16, BF16, BF16, BF16)
    return pl.pallas_call(
        _inproj_kernel,
        out_shape=tuple(jax.ShapeDtypeStruct((m, wd), dt) for wd, dt in zip(widths, dtypes)),
        grid=(m // tm,),
        in_specs=[
            pl.BlockSpec((tm, D_MODEL), row),
            pl.BlockSpec((1, D_MODEL), c2),
            pl.BlockSpec((D_MODEL, D_IN), c2, pipeline_mode=pl.Buffered(1)),
            pl.BlockSpec((2, 8, A_W), c3),
            pl.BlockSpec((tm, 3 * LANES), lambda i: (i % tiles_per_seq, 0)),
            pl.BlockSpec((B_GROUPS, B_CHUNK, B_CHUNK), c3),
            pl.BlockSpec((B_CHUNK, BRANCH_W), c2),
        ],
        out_specs=tuple(pl.BlockSpec((tm, wd), row) for wd in widths),
        compiler_params=pltpu.CompilerParams(
            dimension_semantics=("arbitrary",), vmem_limit_bytes=VMEM_LIMIT),
        name="inproj",
    )(x, g, w, gc, rope_tab, ws, bs_tile)


def _split3(x):
    hi = x.astype(BF16)
    r1 = x - hi.astype(F32)
    mid = r1.astype(BF16)
    lo = (r1 - mid.astype(F32)).astype(BF16)
    return hi, mid, lo


def _seg_bcast(b, m, pick):
    c = b.shape[0]
    nb = c // (2 * m)
    r = b.reshape(nb, 2 * m, LANES)[:, pick:pick + 1, :]
    return jnp.broadcast_to(r, (nb, 2 * m, LANES)).reshape(c, LANES)


def _scan_stream(q, v, lf, k, st, fwd, row, tri, lv):
    c = q.shape[0]
    hi, mid, lo = _split3(lf)
    b = _dot(tri, hi) + _dot(tri, mid) + _dot(tri, lo)
    b_tot = b[c - 1:c, :] if fwd else b[0:1, :]

    attn = jnp.where(lv == 0, _dot_nt(q, k), 0.0)
    up1 = pltpu.roll(lf, c - 1, axis=0)
    dn1 = pltpu.roll(lf, 1, axis=0)
    m, code = 1, 1
    while m < c:
        second = (row & m) != 0
        t_row = second if fwd else jnp.logical_not(second)
        if m >= 4:
            d = b - _seg_bcast(b, m, m - 1 if fwd else m)
            ex = jnp.where(t_row, d, -d)
        elif m == 2:
            r4 = row & 3
            if fwd:
                ex = jnp.where(r4 == 0, up1, jnp.where(r4 == 1, 0.0, jnp.where(r4 == 2, lf, lf + dn1)))
            else:
                ex = jnp.where(r4 == 0, lf + up1, jnp.where(r4 == 1, lf, jnp.where(r4 == 2, 0.0, dn1)))
        else:
            ex = jnp.where(t_row, lf, 0.0)
        w = jnp.exp2(ex).astype(BF16)
        attn = jnp.where(lv == code, _dot_nt(q * w, k * w), attn)
        m, code = 2 * m, code + 1

    o = _dot(attn.astype(BF16), v) + _dot_nt(q * jnp.exp2(b).astype(BF16), st.astype(BF16))
    k_st = k * jnp.exp2(b_tot - b).astype(BF16)
    st_new = st * jnp.exp2(b_tot) + _dot_tn(v, k_st)
    return o, st_new


def _scan_kernel(qf_ref, vf_ref, lf_ref, kf_ref, qb_ref, vb_ref, lb_ref, kb_ref, tri_ref, lv_ref,
                 of_ref, ob_ref, st_ref):
    @pl.when(pl.program_id(1) == 0)
    def _():
        st_ref[...] = jnp.zeros_like(st_ref)

    c = qf_ref.shape[0]
    row = lax.broadcasted_iota(jnp.int32, (c, LANES), 0)
    streams = ((True, qf_ref, vf_ref, lf_ref, kf_ref, of_ref), (False, qb_ref, vb_ref, lb_ref, kb_ref, ob_ref))
    for d, (fwd, q_ref, v_ref, l_ref, k_ref, o_ref) in enumerate(streams):
        for h in range(A_HEADS):
            sl = slice(h * A_DIM, (h + 1) * A_DIM)
            o, st = _scan_stream(q_ref[:, sl], v_ref[:, sl], l_ref[:, sl], k_ref[:, sl], st_ref[d, h],
                                 fwd, row, tri_ref[d], lv_ref[d])
            o_ref[:, sl] = o
            st_ref[d, h] = st


def _scan_tables(c):
    t = np.arange(c)[:, None]
    s = np.arange(c)[None, :]
    tri = np.stack([s <= t, s >= t]).astype(np.float32)
    x = t ^ s
    level = np.where(x == 0, 0, np.floor(np.log2(np.maximum(x, 1))).astype(np.int64) + 1)
    lv = np.stack([np.where(s <= t, level, -1), np.where(s >= t, level, -1)]).astype(np.int32)
    return jnp.asarray(tri, BF16), jnp.asarray(lv)


def _scan(qv, lf, kk, bsz, seq):
    m = qv.shape[0]
    c = SCAN_CHUNK
    nc = seq // c
    tri, lv = _scan_tables(c)

    def fspec(col):
        return pl.BlockSpec((c, A_W), lambda b, i: (b * nc + i, col))

    def bspec(col):
        return pl.BlockSpec((c, A_W), lambda b, i: (b * nc + nc - 1 - i, col))

    c3 = lambda b, i: (0, 0, 0)
    return pl.pallas_call(
        _scan_kernel,
        out_shape=(jax.ShapeDtypeStruct((m, A_W), F32), jax.ShapeDtypeStruct((m, A_W), F32)),
        grid=(bsz, nc),
        in_specs=[
            fspec(0), fspec(1), fspec(0), fspec(0),
            bspec(0), bspec(1), bspec(1), bspec(1),
            pl.BlockSpec((2, c, c), c3), pl.BlockSpec((2, c, c), c3),
        ],
        out_specs=(fspec(0), bspec(0)),
        scratch_shapes=[pltpu.VMEM((2, A_HEADS, A_DIM, A_DIM), F32)],
        compiler_params=pltpu.CompilerParams(
            dimension_semantics=("arbitrary", "arbitrary"), vmem_limit_bytes=VMEM_LIMIT),
        name="hgrn2_scan",
    )(qv, qv, lf, kk, qv, qv, lf, kk, tri, lv)


def _post_kernel(sink_ref, of_ref, ob_ref, sg_ref, yb_ref, cq_ref, kv_ref, kvp_ref, kvn_ref, gt_ref, x_ref,
                 an_ref, wbr_ref, wout_ref, o_ref, *, tiles_per_seq):
    tm = x_ref.shape[0]
    nsub = tm // C_BLOCK

    ya = jnp.concatenate(
        [_rms(of_ref[:, h * A_DIM:(h + 1) * A_DIM] + ob_ref[:, h * A_DIM:(h + 1) * A_DIM], an_ref[...])
         for h in range(A_HEADS)], axis=1)
    ya = (ya * sg_ref[...].astype(F32)).astype(BF16)
    merged = gt_ref[:, 0:D_MODEL].astype(F32) * _dot(ya, wbr_ref[0])
    merged = merged + gt_ref[:, D_MODEL:2 * D_MODEL].astype(F32) * _dot(yb_ref[...], wbr_ref[1])

    ii = lax.broadcasted_iota(jnp.int32, (C_BLOCK, 3 * C_BLOCK), 0)
    jj = lax.broadcasted_iota(jnp.int32, (C_BLOCK, 3 * C_BLOCK), 1)
    band = (jj >= ii) & (jj <= ii + 2 * C_BLOCK)
    seq_tile = pl.program_id(0) % tiles_per_seq
    has_prev = seq_tile != 0
    has_next = seq_tile != tiles_per_seq - 1
    lo = lax.broadcasted_iota(jnp.int32, (1, LANES), 1) < C_HDIM

    def operand(col):
        sl = slice(col * LANES, (col + 1) * LANES)
        return jnp.concatenate([kvp_ref[:, sl], kv_ref[:, sl], kvn_ref[:, sl]], axis=0)

    operands = [operand(col) for col in range(KV_W // LANES)]
    n_pairs = C_QHEADS // 2
    yc_blocks = []
    for j in range(nsub):
        rows = slice(j * C_BLOCK, (j + 1) * C_BLOCK)
        keys = slice(j * C_BLOCK, (j + 3) * C_BLOCK)
        mask = band
        if j == 0:
            mask = mask & ((jj >= C_BLOCK) | has_prev)
        if j == nsub - 1:
            mask = mask & ((jj < 2 * C_BLOCK) | has_next)
        pairs = []
        for p in range(n_pairs):
            g = p // (n_pairs // 2)
            q_pair = cq_ref[rows, p * LANES:(p + 1) * LANES]
            outs, sinks = [], []
            for half in range(2):
                sink = sink_ref[2 * p + half] * LOG2E
                s = jnp.where(mask, _dot_nt(q_pair, operands[2 * g + half][keys, :]), NEG_BIG)
                mx = jnp.maximum(jnp.max(s, axis=-1, keepdims=True), sink)
                outs.append(_dot(jnp.exp2(s - mx).astype(BF16), operands[4 + 2 * g + half][keys, :]))
                sinks.append(jnp.exp2(sink - mx))
            num = jnp.where(lo, outs[0], outs[1])
            den = pltpu.roll(jnp.where(lo, outs[1], outs[0]), C_HDIM, axis=1) + jnp.where(lo, sinks[0], sinks[1])
            pairs.append((num * (1.0 / den)).astype(BF16))
        yc_blocks.append(jnp.concatenate(pairs, axis=1))
    yc = jnp.concatenate(yc_blocks, axis=0)

    merged = merged + gt_ref[:, 2 * D_MODEL:3 * D_MODEL].astype(F32) * _dot(yc, wbr_ref[2])
    o_ref[...] = x_ref[...] + _dot(merged.astype(BF16), wout_ref[...])


def _post(o_f, o_b, sg, yb, cq, kv, gt, x, sink, a_norm, wbr, wout, seq, tm):
    m = x.shape[0]
    sub = tm // C_BLOCK
    nblk = m // C_BLOCK
    tiles_per_seq = seq // tm
    row = lambda i: (i, 0)
    return pl.pallas_call(
        functools.partial(_post_kernel, tiles_per_seq=tiles_per_seq),
        out_shape=jax.ShapeDtypeStruct((m, D_MODEL), F32),
        grid=(m // tm,),
        in_specs=[
            pl.BlockSpec(memory_space=pltpu.SMEM),
            pl.BlockSpec((tm, A_W), row), pl.BlockSpec((tm, A_W), row), pl.BlockSpec((tm, A_W), row),
            pl.BlockSpec((tm, BRANCH_W), row), pl.BlockSpec((tm, BRANCH_W), row),
            pl.BlockSpec((tm, KV_W), row),
            pl.BlockSpec((C_BLOCK, KV_W), lambda i: (jnp.maximum(i * sub - 1, 0), 0)),
            pl.BlockSpec((C_BLOCK, KV_W), lambda i: (jnp.minimum((i + 1) * sub, nblk - 1), 0)),
            pl.BlockSpec((tm, N_BRANCH * D_MODEL), row),
            pl.BlockSpec((tm, D_MODEL), row),
            pl.BlockSpec((1, A_DIM), lambda i: (0, 0)),
            pl.BlockSpec((N_BRANCH, BRANCH_W, D_MODEL), lambda i: (0, 0, 0)),
            pl.BlockSpec((D_MODEL, D_MODEL), lambda i: (0, 0)),
        ],
        out_specs=pl.BlockSpec((tm, D_MODEL), row),
        compiler_params=pltpu.CompilerParams(
            dimension_semantics=("arbitrary",), vmem_limit_bytes=VMEM_LIMIT),
        name="post",
    )(sink, o_f, o_b, sg, yb, cq, kv, kv, kv, gt, x, a_norm, wbr, wout)


def _mlp_kernel(x_ref, g_ref, wu_ref, wd_ref, fn_ref, o_ref, *, final):
    x = x_ref[...]
    hn = _rms(x, g_ref[...]).astype(BF16)
    h = jnp.maximum(_dot(hn, wu_ref[...]), 0.0)
    x = x + _dot((h * h).astype(BF16), wd_ref[...])
    if final:
        x = _rms(x, fn_ref[...])
    o_ref[...] = x


def _mlp(x, g, wu, wd, fn, final, tm):
    m = x.shape[0]
    const = lambda r: (0, 0)
    return pl.pallas_call(
        functools.partial(_mlp_kernel, final=final),
        out_shape=jax.ShapeDtypeStruct((m, D_MODEL), F32),
        grid=(m // tm,),
        in_specs=[
            pl.BlockSpec((tm, D_MODEL), lambda r: (r, 0)),
            pl.BlockSpec((1, D_MODEL), const),
            pl.BlockSpec((D_MODEL, D_FF), const, pipeline_mode=pl.Buffered(1)),
            pl.BlockSpec((D_FF, D_MODEL), const, pipeline_mode=pl.Buffered(1)),
            pl.BlockSpec((1, D_MODEL), const),
        ],
        out_specs=pl.BlockSpec((tm, D_MODEL), lambda r: (r, 0)),
        compiler_params=pltpu.CompilerParams(
            dimension_semantics=("arbitrary",), vmem_limit_bytes=VMEM_LIMIT),
        name="mlp",
    )(x, g, wu, wd, fn)


def _rope_table(seq):
    half = ROPE_DIM // 2
    inv = ROPE_THETA ** (-jnp.arange(half, dtype=F32) * (2.0 / ROPE_DIM))
    ang = jnp.arange(seq, dtype=F32)[:, None] * inv[None, :]
    cos, sin = jnp.cos(ang), jnp.sin(ang)
    pad = jnp.zeros((seq, C_HDIM - ROPE_DIM), F32)
    zero = jnp.zeros_like(sin)
    cos_h = jnp.concatenate([cos, cos, pad + 1.0], axis=1)
    sin_a = jnp.concatenate([-sin, zero, pad], axis=1)
    sin_b = jnp.concatenate([zero, sin, pad], axis=1)
    reps = LANES // C_HDIM
    return jnp.concatenate([jnp.tile(t, (1, reps)) for t in (cos_h, sin_a, sin_b)], axis=1)


def kernel(x, w_in, ln1, lb_logits, a_norm, w_s, b_s, sink, w_br, w_out, ln2, w_up, w_down, final_norm):
    bsz, seq, _ = x.shape
    depth = w_in.shape[0]
    m = bsz * seq
    tm = min(512, seq)

    p = jax.nn.softmax(lb_logits.astype(F32), axis=0)
    cum = jnp.cumsum(p, axis=0)
    lower = (cum - cum[0:1]).reshape(depth, 2, 1, A_W)
    gate_consts = jnp.concatenate(
        [jnp.log(lower), jnp.log1p(-lower), 1.0 - lower, jnp.zeros((depth, 2, 5, A_W), F32)], axis=2)

    rope_tab = _rope_table(seq)
    q_scale = jnp.ones((D_IN,), F32).at[OFF_CQ:OFF_CQ + BRANCH_W].set(C_HDIM ** -0.5)

    h = x.reshape(m, D_MODEL)
    for l in range(depth):
        w = w_in[l]
        w_perm = (jnp.concatenate([w[:, :ORIG_KV], w[:, ORIG_GZ:], w[:, ORIG_KV:ORIG_GZ]], axis=1)
                  * q_scale).astype(BF16)
        bs_tile = jnp.repeat(b_s[l].T, B_CHUNK, axis=1)
        qv, lf, kk, sg, yb, cq, kv, gt = _inproj(
            h, ln1[l][None, :], w_perm, gate_consts[l], rope_tab, w_s[l].astype(BF16), bs_tile, seq, tm)
        o_f, o_b = _scan(qv, lf, kk, bsz, seq)
        h = _post(o_f, o_b, sg, yb, cq, kv, gt, h, sink[l], a_norm[l][None, :],
                  w_br[l].astype(BF16), w_out[l].astype(BF16), seq, tm)
        h = _mlp(h, ln2[l][None, :], w_up[l].astype(BF16), w_down[l].astype(BF16),
                 final_norm[None, :], l == depth - 1, tm)
    return h.reshape(bsz, seq, D_MODEL)
```

```python
import functools
import math

import numpy as np
import jax
import jax.numpy as jnp
from jax import lax
from jax.experimental import pallas as pl
from jax.experimental.pallas import tpu as pltpu

F32 = jnp.float32
BF16 = jnp.bfloat16

D_MODEL = 1024
A_HEADS = 4
A_DIM = 128
A_W = A_HEADS * A_DIM
B_GROUPS = 4
B_CHUNK = 128
C_QHEADS = 8
C_HDIM = 64
C_BLOCK = 128
ROPE_THETA = 500000.0
ROPE_DIM = C_HDIM // 4
N_BRANCH = 3
D_FF = 4 * D_MODEL
EPS = 1e-6
BRANCH_W = 512

OFF_AQ, OFF_AI, OFF_AFF, OFF_AFB, OFF_AG = 0, 512, 1024, 1536, 2048
OFF_BU, OFF_BV, OFF_CQ, OFF_GZ, OFF_KV = 2560, 3072, 3584, 4096, 7168
D_IN = 7424
ORIG_KV, ORIG_GZ = 4096, 4352

LANES = 128
SCAN_CHUNK = 128
GATE_STAGE_W = 512
EPILOGUE_LAG = 2
KV_W = 8 * LANES
NEG_BIG = -1e30
LOG2E = math.log2(math.e)
VMEM_LIMIT = 56 * 1024 * 1024


def _sigmoid(x):
    return 0.5 * jnp.tanh(0.5 * x) + 0.5


def _rms(x, g):
    return x * lax.rsqrt(jnp.mean(x * x, axis=-1, keepdims=True) + EPS) * g


def _dot(a, b):
    return jnp.dot(a, b, preferred_element_type=F32)


def _dot_nt(a, b):
    return lax.dot_general(a, b, (((1,), (1,)), ((), ())), preferred_element_type=F32)


def _dot_tn(a, b):
    return lax.dot_general(a, b, (((0,), (0,)), ((), ())), preferred_element_type=F32)


def _gelu_tanh(x):
    return 0.5 * x * (1.0 + jnp.tanh(math.sqrt(2.0 / math.pi) * (x + 0.044715 * (x * x * x))))


def _rope(t, tab):
    return (t * tab[:, 0:LANES]
            + pltpu.roll(t, LANES - ROPE_DIM // 2, axis=1) * tab[:, LANES:2 * LANES]
            + pltpu.roll(t, ROPE_DIM // 2, axis=1) * tab[:, 2 * LANES:3 * LANES])


def _inproj_kernel(x_ref, g_ref, w_ref, gc_ref, rp_ref, ws_ref, bs_ref,
                   qv_ref, lf_ref, kk_ref, sg_ref, yb_ref, cq_ref, kv_ref, gt_ref):
    tm = x_ref.shape[0]
    xn = _rms(x_ref[...], g_ref[...]).astype(BF16)

    def proj(off, width):
        return _dot(xn, w_ref[:, off:off + width])

    def epi_qv(off):
        def epi(z):
            qv_ref[:, off:off + A_W] = z.astype(BF16)
        return epi

    def epi_gate(d):
        def epi(z):
            e = jnp.exp2(jnp.abs(z) * (-LOG2E))
            r = 1.0 / (1.0 + e)
            pos = z >= 0
            one_m_lb = gc_ref[d, 1:2, :]
            f = gc_ref[d, 0:1, :] + one_m_lb * (jnp.where(pos, 1.0, e) * r)
            floor = gc_ref[d, 2:3, :] + jnp.minimum(z, 0.0) * LOG2E
            lf_ref[:, d * A_W:(d + 1) * A_W] = jnp.maximum(jnp.log2(f), floor)
            kk_ref[:, d * A_W:(d + 1) * A_W] = (one_m_lb * (jnp.where(pos, e, 1.0) * r)).astype(BF16)
        return epi

    def epi_sg(z):
        sg_ref[...] = (z * _sigmoid(z)).astype(BF16)

    held = {}

    def epi_bu(z):
        held["u"] = _gelu_tanh(z)

    def epi_bv(z):
        u = held["u"]
        v = _gelu_tanh(z)
        vc = v - jnp.mean(v, axis=-1, keepdims=True)
        vn = (vc * lax.rsqrt(jnp.mean(vc * vc, axis=-1, keepdims=True) + EPS)).astype(BF16)
        for j in range(tm // B_CHUNK):
            rows = slice(j * B_CHUNK, (j + 1) * B_CHUNK)
            for grp in range(B_GROUPS):
                sl = slice(grp * LANES, (grp + 1) * LANES)
                mixed = _dot(ws_ref[grp], vn[rows, sl]) + bs_ref[:, sl]
                yb_ref[rows, sl] = (u[rows, sl] * mixed).astype(BF16)

    def epi_cq(z):
        tab = rp_ref[...]
        for p in range(BRANCH_W // LANES):
            sl = slice(p * LANES, (p + 1) * LANES)
            cq_ref[:, sl] = (_rope(z[:, sl], tab) * LOG2E).astype(BF16)

    def epi_kv(z):
        lo = lax.broadcasted_iota(jnp.int32, (1, LANES), 1) < C_HDIM
        for i, (t, fill) in enumerate(((_rope(z[:, 0:LANES], rp_ref[...]), 0.0), (z[:, LANES:], 1.0))):
            sw = pltpu.roll(t, C_HDIM, axis=1)
            placed = (jnp.where(lo, t, fill), jnp.where(lo, fill, sw), jnp.where(lo, sw, fill), jnp.where(lo, fill, t))
            for c, val in enumerate(placed):
                kv_ref[:, (4 * i + c) * LANES:(4 * i + c + 1) * LANES] = val.astype(BF16)

    def epi_gt(off):
        def epi(z):
            gt_ref[:, off:off + GATE_STAGE_W] = _sigmoid(z).astype(BF16)
        return epi

    stages = [(OFF_AFF, A_W, epi_gate(0)), (OFF_AFB, A_W, epi_gate(1)),
              (OFF_AG, A_W, epi_sg), (OFF_BU, BRANCH_W, epi_bu), (OFF_BV, BRANCH_W, epi_bv),
              (OFF_CQ, BRANCH_W, epi_cq), (OFF_KV, 2 * LANES, epi_kv)]
    stages += [(OFF_GZ + off, GATE_STAGE_W, epi_gt(off)) for off in range(0, N_BRANCH * D_MODEL, GATE_STAGE_W)]
    stages += [(OFF_AQ, A_W, epi_qv(0)), (OFF_AI, A_W, epi_qv(A_W))]

    pending = []
    for off, width, epi in stages:
        pending.append((epi, proj(off, width)))
        if len(pending) > EPILOGUE_LAG:
            epi0, z0 = pending.pop(0)
            epi0(z0)
    for epi0, z0 in pending:
        epi0(z0)


def _inproj(x, g, w, gc, rope_tab, ws, bs_tile, seq, tm):
    m = x.shape[0]
    tiles_per_seq = seq // tm
    row = lambda i: (i, 0)
    c2 = lambda i: (0, 0)
    c3 = lambda i: (0, 0, 0)
    widths = (2 * A_W, 2 * A_W, 2 * A_W, A_W, BRANCH_W, BRANCH_W, KV_W, N_BRANCH * D_MODEL)
    dtypes = (BF16, F32, BF16, BF16, BF16, BF16, BF16, BF16)
    return pl.pallas_call(
        _inproj_kernel,
        out_shape=tuple(jax.ShapeDtypeStruct((m, wd), dt) for wd, dt in zip(widths, dtypes)),
        grid=(m // tm,),
        in_specs=[
            pl.BlockSpec((tm, D_MODEL), row),
            pl.BlockSpec((1, D_MODEL), c2),
            pl.BlockSpec((D_MODEL, D_IN), c2, pipeline_mode=pl.Buffered(1)),
            pl.BlockSpec((2, 8, A_W), c3),
            pl.BlockSpec((tm, 3 * LANES), lambda i: (i % tiles_per_seq, 0)),
            pl.BlockSpec((B_GROUPS, B_CHUNK, B_CHUNK), c3),
            pl.BlockSpec((B_CHUNK, BRANCH_W), c2),
        ],
        out_specs=tuple(pl.BlockSpec((tm, wd), row) for wd in widths),
        compiler_params=pltpu.CompilerParams(
            dimension_semantics=("arbitrary",), vmem_limit_bytes=VMEM_LIMIT),
        name="inproj",
    )(x, g, w, gc, rope_tab, ws, bs_tile)


def _split3(x):
    hi = x.astype(BF16)
    r1 = x - hi.astype(F32)
    mid = r1.astype(BF16)
    lo = (r1 - mid.astype(F32)).astype(BF16)
    return hi, mid, lo


def _seg_bcast(b, m, pick):
    c, w = b.shape
    nb = c // (2 * m)
    r = b.reshape(nb, 2 * m, w)[:, pick:pick + 1, :]
    return jnp.broadcast_to(r, (nb, 2 * m, w)).reshape(c, w)


def _per_head(fn):
    return jnp.concatenate([fn(slice(h * A_DIM, (h + 1) * A_DIM), h) for h in range(A_HEADS)], axis=1)


def _scan_direction(q, v, lf, k, st_ref, fwd, row, tri, lv):
    c = q.shape[0]
    hi, mid, lo = _split3(lf)
    b = _dot(tri, hi) + _dot(tri, mid) + _dot(tri, lo)
    b_tot = b[c - 1:c, :] if fwd else b[0:1, :]

    attn = jnp.where(lv == 0, _per_head(lambda sl, h: _dot_nt(q[:, sl], k[:, sl])), 0.0)
    up1 = pltpu.roll(lf, c - 1, axis=0)
    dn1 = pltpu.roll(lf, 1, axis=0)
    m, code = 1, 1
    while m < c:
        second = (row & m) != 0
        t_row = second if fwd else jnp.logical_not(second)
        if m >= 4:
            d = b - _seg_bcast(b, m, m - 1 if fwd else m)
            ex = jnp.where(t_row, d, -d)
        elif m == 2:
            r4 = row & 3
            if fwd:
                ex = jnp.where(r4 == 0, up1, jnp.where(r4 == 1, 0.0, jnp.where(r4 == 2, lf, lf + dn1)))
            else:
                ex = jnp.where(r4 == 0, lf + up1, jnp.where(r4 == 1, lf, jnp.where(r4 == 2, 0.0, dn1)))
        else:
            ex = jnp.where(t_row, lf, 0.0)
        w = jnp.exp2(ex).astype(BF16)
        qw, kw = q * w, k * w
        attn = jnp.where(lv == code, _per_head(lambda sl, h: _dot_nt(qw[:, sl], kw[:, sl])), attn)
        m, code = 2 * m, code + 1

    attn = attn.astype(BF16)
    q_in = q * jnp.exp2(b).astype(BF16)
    k_st = k * jnp.exp2(b_tot - b).astype(BF16)
    decay = jnp.exp2(b_tot)
    o = _per_head(lambda sl, h: _dot(attn[:, sl], v[:, sl]) + _dot_nt(q_in[:, sl], st_ref[h].astype(BF16)))
    for h in range(A_HEADS):
        sl = slice(h * A_DIM, (h + 1) * A_DIM)
        st_ref[h] = st_ref[h] * decay[:, sl] + _dot_tn(v[:, sl], k_st[:, sl])
    return o


def _scan_kernel(qf_ref, vf_ref, lf_ref, kf_ref, qb_ref, vb_ref, lb_ref, kb_ref, tri_ref, lv_ref,
                 of_ref, ob_ref, st_ref):
    @pl.when(pl.program_id(1) == 0)
    def _():
        st_ref[...] = jnp.zeros_like(st_ref)

    row = lax.broadcasted_iota(jnp.int32, lf_ref.shape, 0)
    streams = ((True, qf_ref, vf_ref, lf_ref, kf_ref, of_ref), (False, qb_ref, vb_ref, lb_ref, kb_ref, ob_ref))
    for d, (fwd, q_ref, v_ref, l_ref, k_ref, o_ref) in enumerate(streams):
        o_ref[...] = _scan_direction(q_ref[...], v_ref[...], l_ref[...], k_ref[...], st_ref.at[d],
                                     fwd, row, tri_ref[d], lv_ref[d])


def _scan_tables(c):
    t = np.arange(c)[:, None]
    s = np.arange(c)[None, :]
    tri = np.stack([s <= t, s >= t]).astype(np.float32)
    x = t ^ s
    level = np.where(x == 0, 0, np.floor(np.log2(np.maximum(x, 1))).astype(np.int64) + 1)
    lv = np.stack([np.where(s <= t, level, -1), np.where(s >= t, level, -1)]).astype(np.int32)
    return jnp.asarray(tri, BF16), jnp.asarray(np.tile(lv, (1, 1, A_HEADS)))


def _scan(qv, lf, kk, bsz, seq):
    m = qv.shape[0]
    c = SCAN_CHUNK
    nc = seq // c
    tri, lv = _scan_tables(c)

    def fspec(col):
        return pl.BlockSpec((c, A_W), lambda b, i: (b * nc + i, col))

    def bspec(col):
        return pl.BlockSpec((c, A_W), lambda b, i: (b * nc + nc - 1 - i, col))

    c3 = lambda b, i: (0, 0, 0)
    return pl.pallas_call(
        _scan_kernel,
        out_shape=(jax.ShapeDtypeStruct((m, A_W), F32), jax.ShapeDtypeStruct((m, A_W), F32)),
        grid=(bsz, nc),
        in_specs=[
            fspec(0), fspec(1), fspec(0), fspec(0),
            bspec(0), bspec(1), bspec(1), bspec(1),
            pl.BlockSpec((2, c, c), c3), pl.BlockSpec((2, c, A_HEADS * c), c3),
        ],
        out_specs=(fspec(0), bspec(0)),
        scratch_shapes=[pltpu.VMEM((2, A_HEADS, A_DIM, A_DIM), F32)],
        compiler_params=pltpu.CompilerParams(
            dimension_semantics=("arbitrary", "arbitrary"), vmem_limit_bytes=VMEM_LIMIT),
        name="hgrn2_scan",
    )(qv, qv, lf, kk, qv, qv, lf, kk, tri, lv)


def _post_kernel(sink_ref, of_ref, ob_ref, sg_ref, yb_ref, cq_ref, kv_ref, kvp_ref, kvn_ref, gt_ref, x_ref,
                 an_ref, wbr_ref, wout_ref, o_ref, *, tiles_per_seq):
    tm = x_ref.shape[0]
    nsub = tm // C_BLOCK

    ya = jnp.concatenate(
        [_rms(of_ref[:, h * A_DIM:(h + 1) * A_DIM] + ob_ref[:, h * A_DIM:(h + 1) * A_DIM], an_ref[...])
         for h in range(A_HEADS)], axis=1)
    ya = (ya * sg_ref[...].astype(F32)).astype(BF16)
    merged = gt_ref[:, 0:D_MODEL].astype(F32) * _dot(ya, wbr_ref[0])
    merged = merged + gt_ref[:, D_MODEL:2 * D_MODEL].astype(F32) * _dot(yb_ref[...], wbr_ref[1])

    ii = lax.broadcasted_iota(jnp.int32, (C_BLOCK, 3 * C_BLOCK), 0)
    jj = lax.broadcasted_iota(jnp.int32, (C_BLOCK, 3 * C_BLOCK), 1)
    band = (jj >= ii) & (jj <= ii + 2 * C_BLOCK)
    seq_tile = pl.program_id(0) % tiles_per_seq
    has_prev = seq_tile != 0
    has_next = seq_tile != tiles_per_seq - 1
    lo = lax.broadcasted_iota(jnp.int32, (1, LANES), 1) < C_HDIM

    def operand(col):
        sl = slice(col * LANES, (col + 1) * LANES)
        return jnp.concatenate([kvp_ref[:, sl], kv_ref[:, sl], kvn_ref[:, sl]], axis=0)

    operands = [operand(col) for col in range(KV_W // LANES)]
    n_pairs = C_QHEADS // 2

    scores, sinks, value_ops = [], [], []
    for j in range(nsub):
        rows = slice(j * C_BLOCK, (j + 1) * C_BLOCK)
        keys = slice(j * C_BLOCK, (j + 3) * C_BLOCK)
        mask = band
        if j == 0:
            mask = mask & ((jj >= C_BLOCK) | has_prev)
        if j == nsub - 1:
            mask = mask & ((jj < 2 * C_BLOCK) | has_next)
        for p in range(n_pairs):
            g = p // (n_pairs // 2)
            q_pair = cq_ref[rows, p * LANES:(p + 1) * LANES]
            for half in range(2):
                scores.append(jnp.where(mask, _dot_nt(q_pair, operands[2 * g + half][keys, :]), NEG_BIG))
                sinks.append(sink_ref[2 * p + half] * LOG2E)
                value_ops.append(operands[4 + 2 * g + half][keys, :])
    maxes = [jnp.maximum(jnp.max(s, axis=-1, keepdims=True), sk) for s, sk in zip(scores, sinks)]
    probs = [jnp.exp2(s - mx).astype(BF16) for s, mx in zip(scores, maxes)]
    sink_p = [jnp.exp2(sk - mx) for sk, mx in zip(sinks, maxes)]
    outs = [_dot(pr, vo) for pr, vo in zip(probs, value_ops)]

    yc_blocks = []
    for j in range(nsub):
        pairs = []
        for p in range(n_pairs):
            i0 = (j * n_pairs + p) * 2
            o_lo, o_hi = outs[i0], outs[i0 + 1]
            num = jnp.where(lo, o_lo, o_hi)
            den = (pltpu.roll(jnp.where(lo, o_hi, o_lo), C_HDIM, axis=1)
                   + jnp.where(lo, sink_p[i0], sink_p[i0 + 1]))
            pairs.append((num * (1.0 / den)).astype(BF16))
        yc_blocks.append(jnp.concatenate(pairs, axis=1))
    yc = jnp.concatenate(yc_blocks, axis=0)

    merged = merged + gt_ref[:, 2 * D_MODEL:3 * D_MODEL].astype(F32) * _dot(yc, wbr_ref[2])
    o_ref[...] = x_ref[...] + _dot(merged.astype(BF16), wout_ref[...])


def _post(o_f, o_b, sg, yb, cq, kv, gt, x, sink, a_norm, wbr, wout, seq, tm):
    m = x.shape[0]
    sub = tm // C_BLOCK
    nblk = m // C_BLOCK
    tiles_per_seq = seq // tm
    row = lambda i: (i, 0)
    return pl.pallas_call(
        functools.partial(_post_kernel, tiles_per_seq=tiles_per_seq),
        out_shape=jax.ShapeDtypeStruct((m, D_MODEL), F32),
        grid=(m // tm,),
        in_specs=[
            pl.BlockSpec(memory_space=pltpu.SMEM),
            pl.BlockSpec((tm, A_W), row), pl.BlockSpec((tm, A_W), row), pl.BlockSpec((tm, A_W), row),
            pl.BlockSpec((tm, BRANCH_W), row), pl.BlockSpec((tm, BRANCH_W), row),
            pl.BlockSpec((tm, KV_W), row),
            pl.BlockSpec((C_BLOCK, KV_W), lambda i: (jnp.maximum(i * sub - 1, 0), 0)),
            pl.BlockSpec((C_BLOCK, KV_W), lambda i: (jnp.minimum((i + 1) * sub, nblk - 1), 0)),
            pl.BlockSpec((tm, N_BRANCH * D_MODEL), row),
            pl.BlockSpec((tm, D_MODEL), row),
            pl.BlockSpec((1, A_DIM), lambda i: (0, 0)),
            pl.BlockSpec((N_BRANCH, BRANCH_W, D_MODEL), lambda i: (0, 0, 0)),
            pl.BlockSpec((D_MODEL, D_MODEL), lambda i: (0, 0)),
        ],
        out_specs=pl.BlockSpec((tm, D_MODEL), row),
        compiler_params=pltpu.CompilerParams(
            dimension_semantics=("arbitrary",), vmem_limit_bytes=VMEM_LIMIT),
        name="post",
    )(sink, o_f, o_b, sg, yb, cq, kv, kv, kv, gt, x, a_norm, wbr, wout)


def _mlp_kernel(x_ref, g_ref, wu_ref, wd_ref, fn_ref, o_ref, *, final):
    x = x_ref[...]
    hn = _rms(x, g_ref[...]).astype(BF16)
    h = jnp.maximum(_dot(hn, wu_ref[...]), 0.0)
    x = x + _dot((h * h).astype(BF16), wd_ref[...])
    if final:
        x = _rms(x, fn_ref[...])
    o_ref[...] = x


def _mlp(x, g, wu, wd, fn, final, tm):
    m = x.shape[0]
    const = lambda r: (0, 0)
    return pl.pallas_call(
        functools.partial(_mlp_kernel, final=final),
        out_shape=jax.ShapeDtypeStruct((m, D_MODEL), F32),
        grid=(m // tm,),
        in_specs=[
            pl.BlockSpec((tm, D_MODEL), lambda r: (r, 0)),
            pl.BlockSpec((1, D_MODEL), const),
            pl.BlockSpec((D_MODEL, D_FF), const, pipeline_mode=pl.Buffered(1)),
            pl.BlockSpec((D_FF, D_MODEL), const, pipeline_mode=pl.Buffered(1)),
            pl.BlockSpec((1, D_MODEL), const),
        ],
        out_specs=pl.BlockSpec((tm, D_MODEL), lambda r: (r, 0)),
        compiler_params=pltpu.CompilerParams(
            dimension_semantics=("arbitrary",), vmem_limit_bytes=VMEM_LIMIT),
        name="mlp",
    )(x, g, wu, wd, fn)


def _rope_table(seq):
    half = ROPE_DIM // 2
    inv = ROPE_THETA ** (-jnp.arange(half, dtype=F32) * (2.0 / ROPE_DIM))
    ang = jnp.arange(seq, dtype=F32)[:, None] * inv[None, :]
    cos, sin = jnp.cos(ang), jnp.sin(ang)
    pad = jnp.zeros((seq, C_HDIM - ROPE_DIM), F32)
    zero = jnp.zeros_like(sin)
    cos_h = jnp.concatenate([cos, cos, pad + 1.0], axis=1)
    sin_a = jnp.concatenate([-sin, zero, pad], axis=1)
    sin_b = jnp.concatenate([zero, sin, pad], axis=1)
    reps = LANES // C_HDIM
    return jnp.concatenate([jnp.tile(t, (1, reps)) for t in (cos_h, sin_a, sin_b)], axis=1)


def kernel(x, w_in, ln1, lb_logits, a_norm, w_s, b_s, sink, w_br, w_out, ln2, w_up, w_down, final_norm):
    bsz, seq, _ = x.shape
    depth = w_in.shape[0]
    m = bsz * seq
    tm = min(512, seq)

    p = jax.nn.softmax(lb_logits.astype(F32), axis=0)
    cum = jnp.cumsum(p, axis=0)
    lower = (cum - cum[0:1]).reshape(depth, 2, 1, A_W)
    gate_consts = jnp.concatenate(
        [lower, 1.0 - lower, jnp.log2(1.0 - lower) - 2.0, jnp.zeros((depth, 2, 5, A_W), F32)], axis=2)

    rope_tab = _rope_table(seq)
    q_scale = jnp.ones((D_IN,), F32).at[OFF_CQ:OFF_CQ + BRANCH_W].set(C_HDIM ** -0.5)

    h = x.reshape(m, D_MODEL)
    for l in range(depth):
        w = w_in[l]
        w_perm = (jnp.concatenate([w[:, :ORIG_KV], w[:, ORIG_GZ:], w[:, ORIG_KV:ORIG_GZ]], axis=1)
                  * q_scale).astype(BF16)
        bs_tile = jnp.repeat(b_s[l].T, B_CHUNK, axis=1)
        qv, lf, kk, sg, yb, cq, kv, gt = _inproj(
            h, ln1[l][None, :], w_perm, gate_consts[l], rope_tab, w_s[l].astype(BF16), bs_tile, seq, tm)
        o_f, o_b = _scan(qv, lf, kk, bsz, seq)
        h = _post(o_f, o_b, sg, yb, cq, kv, gt, h, sink[l], a_norm[l][None, :],
                  w_br[l].astype(BF16), w_out[l].astype(BF16), seq, tm)
        h = _mlp(h, ln2[l][None, :], w_up[l].astype(BF16), w_down[l].astype(BF16),
                 final_norm[None, :], l == depth - 1, tm)
    return h.reshape(bsz, seq, D_MODEL)
```

```python
import functools
import math

import numpy as np
import jax
import jax.numpy as jnp
from jax import lax
from jax.experimental import pallas as pl
from jax.experimental.pallas import tpu as pltpu

F32 = jnp.float32
BF16 = jnp.bfloat16

D_MODEL = 1024
A_HEADS = 4
A_DIM = 128
A_W = A_HEADS * A_DIM
B_GROUPS = 4
B_CHUNK = 128
C_QHEADS = 8
C_HDIM = 64
C_BLOCK = 128
ROPE_THETA = 500000.0
ROPE_DIM = C_HDIM // 4
N_BRANCH = 3
D_FF = 4 * D_MODEL
EPS = 1e-6
BRANCH_W = 512

OFF_AQ, OFF_AI, OFF_AFF, OFF_AFB, OFF_AG = 0, 512, 1024, 1536, 2048
OFF_BU, OFF_BV, OFF_CQ, OFF_KV, OFF_GZ = 2560, 3072, 3584, 4096, 4352
D_IN = 7424

LANES = 128
SCAN_CHUNK = 128
SCAN_BLOCK = 512
GATE_STAGE_W = 512
EPILOGUE_LAG = 2
KV_W = 8 * LANES
NEG_BIG = -1e30
LOG2E = math.log2(math.e)
VMEM_LIMIT = 56 * 1024 * 1024


def _sigmoid(x):
    return 0.5 * jnp.tanh(0.5 * x) + 0.5


def _rms(x, g):
    return x * lax.rsqrt(jnp.mean(x * x, axis=-1, keepdims=True) + EPS) * g


def _dot(a, b):
    return jnp.dot(a, b, preferred_element_type=F32)


def _dot_nt(a, b):
    return lax.dot_general(a, b, (((1,), (1,)), ((), ())), preferred_element_type=F32)


def _dot_tn(a, b):
    return lax.dot_general(a, b, (((0,), (0,)), ((), ())), preferred_element_type=F32)


def _gelu_tanh(x):
    return 0.5 * x * (1.0 + jnp.tanh(math.sqrt(2.0 / math.pi) * (x + 0.044715 * (x * x * x))))


def _rope(t, tab):
    return (t * tab[:, 0:LANES]
            + pltpu.roll(t, LANES - ROPE_DIM // 2, axis=1) * tab[:, LANES:2 * LANES]
            + pltpu.roll(t, ROPE_DIM // 2, axis=1) * tab[:, 2 * LANES:3 * LANES])


def _inproj_kernel(x_ref, g_ref, w_ref, gc_ref, rp_ref, ws_ref, bs_ref,
                   qv_ref, lf_ref, kk_ref, sg_ref, yb_ref, cq_ref, kv_ref, gt_ref):
    tm = x_ref.shape[0]
    xn = _rms(x_ref[...], g_ref[...]).astype(BF16)

    def proj(off, width):
        return _dot(xn, w_ref[:, off:off + width])

    def epi_qv(off):
        def epi(z):
            qv_ref[:, off:off + A_W] = z.astype(BF16)
        return epi

    def epi_gate(d):
        def epi(z):
            e = jnp.exp2(jnp.abs(z) * (-LOG2E))
            r = 1.0 / (1.0 + e)
            pos = z >= 0
            one_m_lb = gc_ref[d, 1:2, :]
            f = gc_ref[d, 0:1, :] + one_m_lb * (jnp.where(pos, 1.0, e) * r)
            floor = gc_ref[d, 2:3, :] + jnp.minimum(z, 0.0) * LOG2E
            lf_ref[:, d * A_W:(d + 1) * A_W] = jnp.maximum(jnp.log2(f), floor)
            kk_ref[:, d * A_W:(d + 1) * A_W] = (one_m_lb * (jnp.where(pos, e, 1.0) * r)).astype(BF16)
        return epi

    def epi_sg(z):
        sg_ref[...] = (z * _sigmoid(z)).astype(BF16)

    held = {}

    def epi_bu(z):
        held["u"] = _gelu_tanh(z)

    def epi_bv(z):
        u = held["u"]
        v = _gelu_tanh(z)
        vc = v - jnp.mean(v, axis=-1, keepdims=True)
        vn = (vc * lax.rsqrt(jnp.mean(vc * vc, axis=-1, keepdims=True) + EPS)).astype(BF16)
        for j in range(tm // B_CHUNK):
            rows = slice(j * B_CHUNK, (j + 1) * B_CHUNK)
            for grp in range(B_GROUPS):
                sl = slice(grp * LANES, (grp + 1) * LANES)
                mixed = _dot(ws_ref[grp], vn[rows, sl]) + bs_ref[:, sl]
                yb_ref[rows, sl] = (u[rows, sl] * mixed).astype(BF16)

    def epi_cq(z):
        tab = rp_ref[...]
        for p in range(BRANCH_W // LANES):
            sl = slice(p * LANES, (p + 1) * LANES)
            cq_ref[:, sl] = (_rope(z[:, sl], tab) * (LOG2E * C_HDIM ** -0.5)).astype(BF16)

    def epi_kv(z):
        lo = lax.broadcasted_iota(jnp.int32, (1, LANES), 1) < C_HDIM
        for i, (t, fill) in enumerate(((_rope(z[:, 0:LANES], rp_ref[...]), 0.0), (z[:, LANES:], 1.0))):
            sw = pltpu.roll(t, C_HDIM, axis=1)
            placed = (jnp.where(lo, t, fill), jnp.where(lo, fill, sw), jnp.where(lo, sw, fill), jnp.where(lo, fill, t))
            for c, val in enumerate(placed):
                kv_ref[:, (4 * i + c) * LANES:(4 * i + c + 1) * LANES] = val.astype(BF16)

    def epi_gt(off):
        def epi(z):
            gt_ref[:, off:off + GATE_STAGE_W] = _sigmoid(z).astype(BF16)
        return epi

    heavy = [(OFF_AFF, A_W, epi_gate(0)), (OFF_AFB, A_W, epi_gate(1)), (OFF_BU, BRANCH_W, epi_bu),
             (OFF_BV, BRANCH_W, epi_bv), (OFF_CQ, BRANCH_W, epi_cq), (OFF_AG, A_W, epi_sg)]
    light = [(OFF_GZ + off, GATE_STAGE_W, epi_gt(off)) for off in range(0, N_BRANCH * D_MODEL, GATE_STAGE_W)]
    stages = [st for pair in zip(heavy, light) for st in pair]
    stages += [(OFF_KV, 2 * LANES, epi_kv), (OFF_AQ, A_W, epi_qv(0)), (OFF_AI, A_W, epi_qv(A_W))]

    pending = []
    for off, width, epi in stages:
        pending.append((epi, proj(off, width)))
        if len(pending) > EPILOGUE_LAG:
            epi0, z0 = pending.pop(0)
            epi0(z0)
    for epi0, z0 in pending:
        epi0(z0)


def _inproj(x, g, w, gc, rope_tab, ws, bs_tile, seq, tm):
    m = x.shape[0]
    tiles_per_seq = seq // tm
    row = lambda i: (i, 0)
    c2 = lambda i: (0, 0)
    c3 = lambda i: (0, 0, 0)
    widths = (2 * A_W, 2 * A_W, 2 * A_W, A_W, BRANCH_W, BRANCH_W, KV_W, N_BRANCH * D_MODEL)
    dtypes = (BF16, F32, BF16, BF16, BF16, BF16, BF16, BF16)
    return pl.pallas_call(
        _inproj_kernel,
        out_shape=tuple(jax.ShapeDtypeStruct((m, wd), dt) for wd, dt in zip(widths, dtypes)),
        grid=(m // tm,),
        in_specs=[
            pl.BlockSpec((tm, D_MODEL), row),
            pl.BlockSpec((1, D_MODEL), c2),
            pl.BlockSpec((D_MODEL, D_IN), c2, pipeline_mode=pl.Buffered(1)),
            pl.BlockSpec((2, 8, A_W), c3),
            pl.BlockSpec((tm, 3 * LANES), lambda i: (i % tiles_per_seq, 0)),
            pl.BlockSpec((B_GROUPS, B_CHUNK, B_CHUNK), c3),
            pl.BlockSpec((B_CHUNK, BRANCH_W), c2),
        ],
        out_specs=tuple(pl.BlockSpec((tm, wd), row) for wd in widths),
        compiler_params=pltpu.CompilerParams(
            dimension_semantics=("arbitrary",), vmem_limit_bytes=VMEM_LIMIT),
        name="inproj",
    )(x, g, w, gc, rope_tab, ws, bs_tile)


def _split3(x):
    hi = x.astype(BF16)
    r1 = x - hi.astype(F32)
    mid = r1.astype(BF16)
    lo = (r1 - mid.astype(F32)).astype(BF16)
    return hi, mid, lo


def _seg_bcast(b, m, pick):
    c, w = b.shape
    nb = c // (2 * m)
    r = b.reshape(nb, 2 * m, w)[:, pick:pick + 1, :]
    return jnp.broadcast_to(r, (nb, 2 * m, w)).reshape(c, w)


def _per_head(fn):
    return jnp.concatenate([fn(slice(h * A_DIM, (h + 1) * A_DIM), h) for h in range(A_HEADS)], axis=1)


def _scan_direction(q, v, lf, k, st_ref, fwd, row, tri, lv):
    c = q.shape[0]
    hi, mid, lo = _split3(lf)
    b = _dot(tri, hi) + _dot(tri, mid) + _dot(tri, lo)
    b_tot = b[c - 1:c, :] if fwd else b[0:1, :]

    attn = jnp.where(lv == 0, _per_head(lambda sl, h: _dot_nt(q[:, sl], k[:, sl])), 0.0)
    up1 = pltpu.roll(lf, c - 1, axis=0)
    dn1 = pltpu.roll(lf, 1, axis=0)
    m, code = 1, 1
    while m < c:
        second = (row & m) != 0
        t_row = second if fwd else jnp.logical_not(second)
        if m >= 4:
            d = b - _seg_bcast(b, m, m - 1 if fwd else m)
            ex = jnp.where(t_row, d, -d)
        elif m == 2:
            r4 = row & 3
            if fwd:
                ex = jnp.where(r4 == 0, up1, jnp.where(r4 == 1, 0.0, jnp.where(r4 == 2, lf, lf + dn1)))
            else:
                ex = jnp.where(r4 == 0, lf + up1, jnp.where(r4 == 1, lf, jnp.where(r4 == 2, 0.0, dn1)))
        else:
            ex = jnp.where(t_row, lf, 0.0)
        w = jnp.exp2(ex).astype(BF16)
        qw, kw = q * w, k * w
        attn = jnp.where(lv == code, _per_head(lambda sl, h: _dot_nt(qw[:, sl], kw[:, sl])), attn)
        m, code = 2 * m, code + 1

    attn = attn.astype(BF16)
    q_in = q * jnp.exp2(b).astype(BF16)
    k_st = k * jnp.exp2(b_tot - b).astype(BF16)
    decay = jnp.exp2(b_tot)
    o = _per_head(lambda sl, h: _dot(attn[:, sl], v[:, sl]) + _dot_nt(q_in[:, sl], st_ref[h].astype(BF16)))
    for h in range(A_HEADS):
        sl = slice(h * A_DIM, (h + 1) * A_DIM)
        st_ref[h] = st_ref[h] * decay[:, sl] + _dot_tn(v[:, sl], k_st[:, sl])
    return o


def _scan_kernel(qf_ref, vf_ref, lf_ref, kf_ref, qb_ref, vb_ref, lb_ref, kb_ref, tri_ref, lv_ref,
                 of_ref, ob_ref, st_ref):
    @pl.when(pl.program_id(1) == 0)
    def _():
        st_ref[...] = jnp.zeros_like(st_ref)

    c = SCAN_CHUNK
    n_sub = lf_ref.shape[0] // c
    row = lax.broadcasted_iota(jnp.int32, (c, lf_ref.shape[1]), 0)
    streams = ((True, qf_ref, vf_ref, lf_ref, kf_ref, of_ref), (False, qb_ref, vb_ref, lb_ref, kb_ref, ob_ref))
    for u in range(n_sub):
        for d, (fwd, q_ref, v_ref, l_ref, k_ref, o_ref) in enumerate(streams):
            rows = pl.ds((u if fwd else n_sub - 1 - u) * c, c)
            o_ref[rows, :] = _scan_direction(q_ref[rows, :], v_ref[rows, :], l_ref[rows, :], k_ref[rows, :],
                                             st_ref.at[d], fwd, row, tri_ref[d], lv_ref[d])


def _scan_tables(c):
    t = np.arange(c)[:, None]
    s = np.arange(c)[None, :]
    tri = np.stack([s <= t, s >= t]).astype(np.float32)
    x = t ^ s
    level = np.where(x == 0, 0, np.floor(np.log2(np.maximum(x, 1))).astype(np.int64) + 1)
    lv = np.stack([np.where(s <= t, level, -1), np.where(s >= t, level, -1)]).astype(np.int32)
    return jnp.asarray(tri, BF16), jnp.asarray(np.tile(lv, (1, 1, A_HEADS)))


def _scan(qv, lf, kk, bsz, seq):
    m = qv.shape[0]
    c = SCAN_CHUNK
    rows = min(SCAN_BLOCK, seq)
    nc = seq // rows
    tri, lv = _scan_tables(c)

    def fspec(col):
        return pl.BlockSpec((rows, A_W), lambda b, i: (b * nc + i, col))

    def bspec(col):
        return pl.BlockSpec((rows, A_W), lambda b, i: (b * nc + nc - 1 - i, col))

    c3 = lambda b, i: (0, 0, 0)
    return pl.pallas_call(
        _scan_kernel,
        out_shape=(jax.ShapeDtypeStruct((m, A_W), F32), jax.ShapeDtypeStruct((m, A_W), F32)),
        grid=(bsz, nc),
        in_specs=[
            fspec(0), fspec(1), fspec(0), fspec(0),
            bspec(0), bspec(1), bspec(1), bspec(1),
            pl.BlockSpec((2, c, c), c3), pl.BlockSpec((2, c, A_HEADS * c), c3),
        ],
        out_specs=(fspec(0), bspec(0)),
        scratch_shapes=[pltpu.VMEM((2, A_HEADS, A_DIM, A_DIM), F32)],
        compiler_params=pltpu.CompilerParams(
            dimension_semantics=("arbitrary", "arbitrary"), vmem_limit_bytes=VMEM_LIMIT),
        name="hgrn2_scan",
    )(qv, qv, lf, kk, qv, qv, lf, kk, tri, lv)


def _post_kernel(sink_ref, of_ref, ob_ref, sg_ref, yb_ref, cq_ref, kv_ref, kvp_ref, kvn_ref, gt_ref, x_ref,
                 an_ref, wbr_ref, wout_ref, o_ref, *, tiles_per_seq):
    tm = x_ref.shape[0]
    nsub = tm // C_BLOCK

    ya = jnp.concatenate(
        [_rms(of_ref[:, h * A_DIM:(h + 1) * A_DIM] + ob_ref[:, h * A_DIM:(h + 1) * A_DIM], an_ref[...])
         for h in range(A_HEADS)], axis=1)
    ya = (ya * sg_ref[...].astype(F32)).astype(BF16)
    merged = gt_ref[:, 0:D_MODEL].astype(F32) * _dot(ya, wbr_ref[0])
    merged = merged + gt_ref[:, D_MODEL:2 * D_MODEL].astype(F32) * _dot(yb_ref[...], wbr_ref[1])

    ii = lax.broadcasted_iota(jnp.int32, (C_BLOCK, 3 * C_BLOCK), 0)
    jj = lax.broadcasted_iota(jnp.int32, (C_BLOCK, 3 * C_BLOCK), 1)
    band = (jj >= ii) & (jj <= ii + 2 * C_BLOCK)
    seq_tile = pl.program_id(0) % tiles_per_seq
    has_prev = seq_tile != 0
    has_next = seq_tile != tiles_per_seq - 1
    lo = lax.broadcasted_iota(jnp.int32, (1, LANES), 1) < C_HDIM

    def operand(col):
        sl = slice(col * LANES, (col + 1) * LANES)
        return jnp.concatenate([kvp_ref[:, sl], kv_ref[:, sl], kvn_ref[:, sl]], axis=0)

    operands = [operand(col) for col in range(KV_W // LANES)]
    n_pairs = C_QHEADS // 2

    scores, sinks, value_ops = [], [], []
    for j in range(nsub):
        rows = slice(j * C_BLOCK, (j + 1) * C_BLOCK)
        keys = slice(j * C_BLOCK, (j + 3) * C_BLOCK)
        mask = band
        if j == 0:
            mask = mask & ((jj >= C_BLOCK) | has_prev)
        if j == nsub - 1:
            mask = mask & ((jj < 2 * C_BLOCK) | has_next)
        for p in range(n_pairs):
            g = p // (n_pairs // 2)
            q_pair = cq_ref[rows, p * LANES:(p + 1) * LANES]
            for half in range(2):
                scores.append(jnp.where(mask, _dot_nt(q_pair, operands[2 * g + half][keys, :]), NEG_BIG))
                sinks.append(sink_ref[2 * p + half] * LOG2E)
                value_ops.append(operands[4 + 2 * g + half][keys, :])
    maxes = [jnp.maximum(jnp.max(s, axis=-1, keepdims=True), sk) for s, sk in zip(scores, sinks)]
    probs = [jnp.exp2(s - mx).astype(BF16) for s, mx in zip(scores, maxes)]
    sink_p = [jnp.exp2(sk - mx) for sk, mx in zip(sinks, maxes)]
    outs = [_dot(pr, vo) for pr, vo in zip(probs, value_ops)]

    yc_blocks = []
    for j in range(nsub):
        pairs = []
        for p in range(n_pairs):
            i0 = (j * n_pairs + p) * 2
            o_lo, o_hi = outs[i0], outs[i0 + 1]
            num = jnp.where(lo, o_lo, o_hi)
            den = (pltpu.roll(jnp.where(lo, o_hi, o_lo), C_HDIM, axis=1)
                   + jnp.where(lo, sink_p[i0], sink_p[i0 + 1]))
            pairs.append((num * (1.0 / den)).astype(BF16))
        yc_blocks.append(jnp.concatenate(pairs, axis=1))
    yc = jnp.concatenate(yc_blocks, axis=0)

    merged = merged + gt_ref[:, 2 * D_MODEL:3 * D_MODEL].astype(F32) * _dot(yc, wbr_ref[2])
    o_ref[...] = x_ref[...] + _dot(merged.astype(BF16), wout_ref[...])


def _post(o_f, o_b, sg, yb, cq, kv, gt, x, sink, a_norm, wbr, wout, seq, tm):
    m = x.shape[0]
    sub = tm // C_BLOCK
    nblk = m // C_BLOCK
    tiles_per_seq = seq // tm
    row = lambda i: (i, 0)
    return pl.pallas_call(
        functools.partial(_post_kernel, tiles_per_seq=tiles_per_seq),
        out_shape=jax.ShapeDtypeStruct((m, D_MODEL), F32),
        grid=(m // tm,),
        in_specs=[
            pl.BlockSpec(memory_space=pltpu.SMEM),
            pl.BlockSpec((tm, A_W), row), pl.BlockSpec((tm, A_W), row), pl.BlockSpec((tm, A_W), row),
            pl.BlockSpec((tm, BRANCH_W), row), pl.BlockSpec((tm, BRANCH_W), row),
            pl.BlockSpec((tm, KV_W), row),
            pl.BlockSpec((C_BLOCK, KV_W), lambda i: (jnp.maximum(i * sub - 1, 0), 0)),
            pl.BlockSpec((C_BLOCK, KV_W), lambda i: (jnp.minimum((i + 1) * sub, nblk - 1), 0)),
            pl.BlockSpec((tm, N_BRANCH * D_MODEL), row),
            pl.BlockSpec((tm, D_MODEL), row),
            pl.BlockSpec((1, A_DIM), lambda i: (0, 0)),
            pl.BlockSpec((N_BRANCH, BRANCH_W, D_MODEL), lambda i: (0, 0, 0)),
            pl.BlockSpec((D_MODEL, D_MODEL), lambda i: (0, 0)),
        ],
        out_specs=pl.BlockSpec((tm, D_MODEL), row),
        compiler_params=pltpu.CompilerParams(
            dimension_semantics=("arbitrary",), vmem_limit_bytes=VMEM_LIMIT),
        name="post",
    )(sink, o_f, o_b, sg, yb, cq, kv, kv, kv, gt, x, a_norm, wbr, wout)


def _mlp_kernel(x_ref, g_ref, wu_ref, wd_ref, fn_ref, o_ref, *, final):
    x = x_ref[...]
    hn = _rms(x, g_ref[...]).astype(BF16)
    h = jnp.maximum(_dot(hn, wu_ref[...]), 0.0)
    x = x + _dot((h * h).astype(BF16), wd_ref[...])
    if final:
        x = _rms(x, fn_ref[...])
    o_ref[...] = x


def _mlp(x, g, wu, wd, fn, final, tm):
    m = x.shape[0]
    const = lambda r: (0, 0)
    return pl.pallas_call(
        functools.partial(_mlp_kernel, final=final),
        out_shape=jax.ShapeDtypeStruct((m, D_MODEL), F32),
        grid=(m // tm,),
        in_specs=[
            pl.BlockSpec((tm, D_MODEL), lambda r: (r, 0)),
            pl.BlockSpec((1, D_MODEL), const),
            pl.BlockSpec((D_MODEL, D_FF), const, pipeline_mode=pl.Buffered(1)),
            pl.BlockSpec((D_FF, D_MODEL), const, pipeline_mode=pl.Buffered(1)),
            pl.BlockSpec((1, D_MODEL), const),
        ],
        out_specs=pl.BlockSpec((tm, D_MODEL), lambda r: (r, 0)),
        compiler_params=pltpu.CompilerParams(
            dimension_semantics=("arbitrary",), vmem_limit_bytes=VMEM_LIMIT),
        name="mlp",
    )(x, g, wu, wd, fn)


def _rope_table(seq):
    half = ROPE_DIM // 2
    inv = ROPE_THETA ** (-jnp.arange(half, dtype=F32) * (2.0 / ROPE_DIM))
    ang = jnp.arange(seq, dtype=F32)[:, None] * inv[None, :]
    cos, sin = jnp.cos(ang), jnp.sin(ang)
    pad = jnp.zeros((seq, C_HDIM - ROPE_DIM), F32)
    zero = jnp.zeros_like(sin)
    cos_h = jnp.concatenate([cos, cos, pad + 1.0], axis=1)
    sin_a = jnp.concatenate([-sin, zero, pad], axis=1)
    sin_b = jnp.concatenate([zero, sin, pad], axis=1)
    reps = LANES // C_HDIM
    return jnp.concatenate([jnp.tile(t, (1, reps)) for t in (cos_h, sin_a, sin_b)], axis=1)


def kernel(x, w_in, ln1, lb_logits, a_norm, w_s, b_s, sink, w_br, w_out, ln2, w_up, w_down, final_norm):
    bsz, seq, _ = x.shape
    depth = w_in.shape[0]
    m = bsz * seq
    tm = min(512, seq)

    p = jax.nn.softmax(lb_logits.astype(F32), axis=0)
    cum = jnp.cumsum(p, axis=0)
    lower = (cum - cum[0:1]).reshape(depth, 2, 1, A_W)
    gate_consts = jnp.concatenate(
        [lower, 1.0 - lower, jnp.log2(1.0 - lower) - 2.0, jnp.zeros((depth, 2, 5, A_W), F32)], axis=2)

    rope_tab = _rope_table(seq)

    h = x.reshape(m, D_MODEL)
    for l in range(depth):
        bs_tile = jnp.repeat(b_s[l].T, B_CHUNK, axis=1)
        qv, lf, kk, sg, yb, cq, kv, gt = _inproj(
            h, ln1[l][None, :], w_in[l].astype(BF16), gate_consts[l], rope_tab, w_s[l].astype(BF16), bs_tile,
            seq, tm)
        o_f, o_b = _scan(qv, lf, kk, bsz, seq)
        h = _post(o_f, o_b, sg, yb, cq, kv, gt, h, sink[l], a_norm[l][None, :],
                  w_br[l].astype(BF16), w_out[l].astype(BF16), seq, tm)
        h = _mlp(h, ln2[l][None, :], w_up[l].astype(BF16), w_down[l].astype(BF16),
                 final_norm[None, :], l == depth - 1, tm)
    return h.reshape(bsz, seq, D_MODEL)
```

```python
import functools
import math

import numpy as np
import jax
import jax.numpy as jnp
from jax import lax
from jax.experimental import pallas as pl
from jax.experimental.pallas import tpu as pltpu

F32 = jnp.float32
BF16 = jnp.bfloat16

D_MODEL = 1024
A_HEADS = 4
A_DIM = 128
A_W = A_HEADS * A_DIM
B_GROUPS = 4
B_CHUNK = 128
C_QHEADS = 8
C_HDIM = 64
C_BLOCK = 128
ROPE_THETA = 500000.0
ROPE_DIM = C_HDIM // 4
N_BRANCH = 3
D_FF = 4 * D_MODEL
EPS = 1e-6
BRANCH_W = 512

OFF_AQ, OFF_AI, OFF_AFF, OFF_AFB, OFF_AG = 0, 512, 1024, 1536, 2048
OFF_BU, OFF_BV, OFF_CQ, OFF_KV, OFF_GZ = 2560, 3072, 3584, 4096, 4352
D_IN = 7424

LANES = 128
SCAN_CHUNK = 128
SCAN_BLOCK = 512
GATE_STAGE_W = 512
EPILOGUE_LAG = 2
KV_W = 8 * LANES
NEG_BIG = -1e30
LOG2E = math.log2(math.e)
VMEM_LIMIT = 56 * 1024 * 1024


def _sigmoid(x):
    return 0.5 * jnp.tanh(0.5 * x) + 0.5


def _rms(x, g):
    return x * lax.rsqrt(jnp.mean(x * x, axis=-1, keepdims=True) + EPS) * g


def _dot(a, b):
    return jnp.dot(a, b, preferred_element_type=F32)


def _dot_nt(a, b):
    return lax.dot_general(a, b, (((1,), (1,)), ((), ())), preferred_element_type=F32)


def _dot_tn(a, b):
    return lax.dot_general(a, b, (((0,), (0,)), ((), ())), preferred_element_type=F32)


def _gelu_tanh(x):
    return 0.5 * x * (1.0 + jnp.tanh(math.sqrt(2.0 / math.pi) * (x + 0.044715 * (x * x * x))))


def _rope(t, tab):
    return (t * tab[:, 0:LANES]
            + pltpu.roll(t, LANES - ROPE_DIM // 2, axis=1) * tab[:, LANES:2 * LANES]
            + pltpu.roll(t, ROPE_DIM // 2, axis=1) * tab[:, 2 * LANES:3 * LANES])


def _inproj_kernel(x_ref, g_ref, w_ref, gc_ref, rp_ref, ws_ref, bs_ref,
                   q_ref, v_ref, lff_ref, lfb_ref, kf_ref, kb_ref, sg_ref, yb_ref, cq_ref, kv_ref, gt_ref):
    tm = x_ref.shape[0]
    xn = _rms(x_ref[...], g_ref[...]).astype(BF16)

    def proj(off, width):
        return _dot(xn, w_ref[:, off:off + width])

    def epi_cast(o_ref):
        def epi(z):
            o_ref[...] = z.astype(BF16)
        return epi

    def epi_gate(d, lf_ref, k_ref):
        def epi(z):
            t = jnp.abs(z) * (-LOG2E)
            e = jnp.exp2(t)
            r = 1.0 / (1.0 + e)
            er = e * r
            pos = z >= 0
            one_m_lb = gc_ref[d, 1:2, :]
            f = gc_ref[d, 0:1, :] + one_m_lb * jnp.where(pos, r, er)
            lf_ref[...] = jnp.maximum(jnp.log2(f), gc_ref[d, 2:3, :] + t)
            k_ref[...] = (one_m_lb * jnp.where(pos, er, r)).astype(BF16)
        return epi

    def epi_sg(z):
        sg_ref[...] = (z * _sigmoid(z)).astype(BF16)

    held = {}

    def epi_bu(z):
        held["u"] = _gelu_tanh(z)

    def epi_bv(z):
        u = held["u"]
        v = _gelu_tanh(z)
        vc = v - jnp.mean(v, axis=-1, keepdims=True)
        vn = (vc * lax.rsqrt(jnp.mean(vc * vc, axis=-1, keepdims=True) + EPS)).astype(BF16)
        for j in range(tm // B_CHUNK):
            rows = slice(j * B_CHUNK, (j + 1) * B_CHUNK)
            for grp in range(B_GROUPS):
                sl = slice(grp * LANES, (grp + 1) * LANES)
                mixed = _dot(ws_ref[grp], vn[rows, sl]) + bs_ref[:, sl]
                yb_ref[rows, sl] = (u[rows, sl] * mixed).astype(BF16)

    def epi_cq(z):
        tab = rp_ref[...]
        for p in range(BRANCH_W // LANES):
            sl = slice(p * LANES, (p + 1) * LANES)
            cq_ref[:, sl] = (_rope(z[:, sl], tab) * (LOG2E * C_HDIM ** -0.5)).astype(BF16)

    def epi_kv(z):
        lo = lax.broadcasted_iota(jnp.int32, (1, LANES), 1) < C_HDIM
        for i, (t, fill) in enumerate(((_rope(z[:, 0:LANES], rp_ref[...]), 0.0), (z[:, LANES:], 1.0))):
            sw = pltpu.roll(t, C_HDIM, axis=1)
            placed = (jnp.where(lo, t, fill), jnp.where(lo, fill, sw), jnp.where(lo, sw, fill), jnp.where(lo, fill, t))
            for c, val in enumerate(placed):
                kv_ref[:, (4 * i + c) * LANES:(4 * i + c + 1) * LANES] = val.astype(BF16)

    def epi_gt(off):
        def epi(z):
            gt_ref[:, off:off + GATE_STAGE_W] = _sigmoid(z).astype(BF16)
        return epi

    heavy = [(OFF_AFF, A_W, epi_gate(0, lff_ref, kf_ref)), (OFF_AFB, A_W, epi_gate(1, lfb_ref, kb_ref)),
             (OFF_BU, BRANCH_W, epi_bu),
             (OFF_BV, BRANCH_W, epi_bv), (OFF_CQ, BRANCH_W, epi_cq), (OFF_AG, A_W, epi_sg)]
    light = [(OFF_GZ + off, GATE_STAGE_W, epi_gt(off)) for off in range(0, N_BRANCH * D_MODEL, GATE_STAGE_W)]
    stages = [st for pair in zip(heavy, light) for st in pair]
    stages += [(OFF_KV, 2 * LANES, epi_kv), (OFF_AQ, A_W, epi_cast(q_ref)), (OFF_AI, A_W, epi_cast(v_ref))]

    pending = []
    for off, width, epi in stages:
        pending.append((epi, proj(off, width)))
        if len(pending) > EPILOGUE_LAG:
            epi0, z0 = pending.pop(0)
            epi0(z0)
    for epi0, z0 in pending:
        epi0(z0)


def _inproj(x, g, w, gc, rope_tab, ws, bs_tile, seq, tm):
    m = x.shape[0]
    tiles_per_seq = seq // tm
    row = lambda i: (i, 0)
    c2 = lambda i: (0, 0)
    c3 = lambda i: (0, 0, 0)
    widths = (A_W,) * 7 + (BRANCH_W, BRANCH_W, KV_W, N_BRANCH * D_MODEL)
    dtypes = (BF16, BF16, F32, F32) + (BF16,) * 7
    return pl.pallas_call(
        _inproj_kernel,
        out_shape=tuple(jax.ShapeDtypeStruct((m, wd), dt) for wd, dt in zip(widths, dtypes)),
        grid=(m // tm,),
        in_specs=[
            pl.BlockSpec((tm, D_MODEL), row),
            pl.BlockSpec((1, D_MODEL), c2),
            pl.BlockSpec((D_MODEL, D_IN), c2, pipeline_mode=pl.Buffered(1)),
            pl.BlockSpec((2, 8, A_W), c3),
            pl.BlockSpec((tm, 3 * LANES), lambda i: (i % tiles_per_seq, 0)),
            pl.BlockSpec((B_GROUPS, B_CHUNK, B_CHUNK), c3),
            pl.BlockSpec((B_CHUNK, BRANCH_W), c2),
        ],
        out_specs=tuple(pl.BlockSpec((tm, wd), row) for wd in widths),
        compiler_params=pltpu.CompilerParams(
            dimension_semantics=("arbitrary",), vmem_limit_bytes=VMEM_LIMIT),
        name="inproj",
    )(x, g, w, gc, rope_tab, ws, bs_tile)


def _split3(x):
    hi = x.astype(BF16)
    r1 = x - hi.astype(F32)
    mid = r1.astype(BF16)
    lo = (r1 - mid.astype(F32)).astype(BF16)
    return hi, mid, lo


def _seg_bcast(b, m, pick):
    c, w = b.shape
    nb = c // (2 * m)
    r = b.reshape(nb, 2 * m, w)[:, pick:pick + 1, :]
    return jnp.broadcast_to(r, (nb, 2 * m, w)).reshape(c, w)


def _per_head(fn):
    return jnp.concatenate([fn(slice(h * A_DIM, (h + 1) * A_DIM), h) for h in range(A_HEADS)], axis=1)


def _scan_direction(q, v, lf, k, st_ref, fwd, row, tri, lv):
    c = q.shape[0]
    hi, mid, lo = _split3(lf)
    b = _dot(tri, hi) + _dot(tri, mid) + _dot(tri, lo)
    b_tot = b[c - 1:c, :] if fwd else b[0:1, :]

    attn = jnp.where(lv == 0, _per_head(lambda sl, h: _dot_nt(q[:, sl], k[:, sl])), 0.0)
    up1 = pltpu.roll(lf, c - 1, axis=0)
    dn1 = pltpu.roll(lf, 1, axis=0)
    m, code = 1, 1
    while m < c:
        second = (row & m) != 0
        t_row = second if fwd else jnp.logical_not(second)
        if m >= 4:
            d = b - _seg_bcast(b, m, m - 1 if fwd else m)
            ex = jnp.where(t_row, d, -d)
        elif m == 2:
            r4 = row & 3
            if fwd:
                ex = jnp.where(r4 == 0, up1, jnp.where(r4 == 1, 0.0, jnp.where(r4 == 2, lf, lf + dn1)))
            else:
                ex = jnp.where(r4 == 0, lf + up1, jnp.where(r4 == 1, lf, jnp.where(r4 == 2, 0.0, dn1)))
        else:
            ex = jnp.where(t_row, lf, 0.0)
        w = jnp.exp2(ex).astype(BF16)
        qw, kw = q * w, k * w
        attn = jnp.where(lv == code, _per_head(lambda sl, h: _dot_nt(qw[:, sl], kw[:, sl])), attn)
        m, code = 2 * m, code + 1

    attn = attn.astype(BF16)
    q_in = q * jnp.exp2(b).astype(BF16)
    k_st = k * jnp.exp2(b_tot - b).astype(BF16)
    decay = jnp.exp2(b_tot)
    o = _per_head(lambda sl, h: _dot(attn[:, sl], v[:, sl]) + _dot_nt(q_in[:, sl], st_ref[h].astype(BF16)))
    for h in range(A_HEADS):
        sl = slice(h * A_DIM, (h + 1) * A_DIM)
        st_ref[h] = st_ref[h] * decay[:, sl] + _dot_tn(v[:, sl], k_st[:, sl])
    return o


def _scan_kernel(qf_ref, vf_ref, lf_ref, kf_ref, qb_ref, vb_ref, lb_ref, kb_ref, tri_ref, lv_ref,
                 of_ref, ob_ref, st_ref):
    @pl.when(pl.program_id(1) == 0)
    def _():
        st_ref[...] = jnp.zeros_like(st_ref)

    c = SCAN_CHUNK
    n_sub = lf_ref.shape[0] // c
    row = lax.broadcasted_iota(jnp.int32, (c, lf_ref.shape[1]), 0)
    streams = ((True, qf_ref, vf_ref, lf_ref, kf_ref, of_ref), (False, qb_ref, vb_ref, lb_ref, kb_ref, ob_ref))
    for u in range(n_sub):
        for d, (fwd, q_ref, v_ref, l_ref, k_ref, o_ref) in enumerate(streams):
            rows = pl.ds((u if fwd else n_sub - 1 - u) * c, c)
            o_ref[rows, :] = _scan_direction(q_ref[rows, :], v_ref[rows, :], l_ref[rows, :], k_ref[rows, :],
                                             st_ref.at[d], fwd, row, tri_ref[d], lv_ref[d])


def _scan_tables(c):
    t = np.arange(c)[:, None]
    s = np.arange(c)[None, :]
    tri = np.stack([s <= t, s >= t]).astype(np.float32)
    x = t ^ s
    level = np.where(x == 0, 0, np.floor(np.log2(np.maximum(x, 1))).astype(np.int64) + 1)
    lv = np.stack([np.where(s <= t, level, -1), np.where(s >= t, level, -1)]).astype(np.int32)
    return jnp.asarray(tri, BF16), jnp.asarray(np.tile(lv, (1, 1, A_HEADS)))


def _scan(q, v, lf_f, lf_b, k_f, k_b, bsz, seq):
    m = q.shape[0]
    c = SCAN_CHUNK
    rows = min(SCAN_BLOCK, seq)
    nc = seq // rows
    tri, lv = _scan_tables(c)
    fspec = pl.BlockSpec((rows, A_W), lambda b, i: (b * nc + i, 0))
    bspec = pl.BlockSpec((rows, A_W), lambda b, i: (b * nc + nc - 1 - i, 0))

    c3 = lambda b, i: (0, 0, 0)
    return pl.pallas_call(
        _scan_kernel,
        out_shape=(jax.ShapeDtypeStruct((m, A_W), F32), jax.ShapeDtypeStruct((m, A_W), F32)),
        grid=(bsz, nc),
        in_specs=[
            fspec, fspec, fspec, fspec, bspec, bspec, bspec, bspec,
            pl.BlockSpec((2, c, c), c3), pl.BlockSpec((2, c, A_HEADS * c), c3),
        ],
        out_specs=(fspec, bspec),
        scratch_shapes=[pltpu.VMEM((2, A_HEADS, A_DIM, A_DIM), F32)],
        compiler_params=pltpu.CompilerParams(
            dimension_semantics=("arbitrary", "arbitrary"), vmem_limit_bytes=VMEM_LIMIT),
        name="hgrn2_scan",
    )(q, v, lf_f, k_f, q, v, lf_b, k_b, tri, lv)


def _post_kernel(sink_ref, of_ref, ob_ref, sg_ref, yb_ref, cq_ref, kv_ref, kvp_ref, kvn_ref, gt_ref, x_ref,
                 an_ref, wbr_ref, wout_ref, o_ref, *, tiles_per_seq):
    tm = x_ref.shape[0]
    nsub = tm // C_BLOCK

    ya = jnp.concatenate(
        [_rms(of_ref[:, h * A_DIM:(h + 1) * A_DIM] + ob_ref[:, h * A_DIM:(h + 1) * A_DIM], an_ref[...])
         for h in range(A_HEADS)], axis=1)
    ya = (ya * sg_ref[...].astype(F32)).astype(BF16)
    merged = gt_ref[:, 0:D_MODEL].astype(F32) * _dot(ya, wbr_ref[0])
    merged = merged + gt_ref[:, D_MODEL:2 * D_MODEL].astype(F32) * _dot(yb_ref[...], wbr_ref[1])

    ii = lax.broadcasted_iota(jnp.int32, (C_BLOCK, 3 * C_BLOCK), 0)
    jj = lax.broadcasted_iota(jnp.int32, (C_BLOCK, 3 * C_BLOCK), 1)
    band = (jj >= ii) & (jj <= ii + 2 * C_BLOCK)
    seq_tile = pl.program_id(0) % tiles_per_seq
    has_prev = seq_tile != 0
    has_next = seq_tile != tiles_per_seq - 1
    lo = lax.broadcasted_iota(jnp.int32, (1, LANES), 1) < C_HDIM

    def operand(col):
        sl = slice(col * LANES, (col + 1) * LANES)
        return jnp.concatenate([kvp_ref[:, sl], kv_ref[:, sl], kvn_ref[:, sl]], axis=0)

    operands = [operand(col) for col in range(KV_W // LANES)]
    n_pairs = C_QHEADS // 2

    scores, sinks, value_ops = [], [], []
    for j in range(nsub):
        rows = slice(j * C_BLOCK, (j + 1) * C_BLOCK)
        keys = slice(j * C_BLOCK, (j + 3) * C_BLOCK)
        mask = band
        if j == 0:
            mask = mask & ((jj >= C_BLOCK) | has_prev)
        if j == nsub - 1:
            mask = mask & ((jj < 2 * C_BLOCK) | has_next)
        for p in range(n_pairs):
            g = p // (n_pairs // 2)
            q_pair = cq_ref[rows, p * LANES:(p + 1) * LANES]
            for half in range(2):
                scores.append(jnp.where(mask, _dot_nt(q_pair, operands[2 * g + half][keys, :]), NEG_BIG))
                sinks.append(sink_ref[2 * p + half] * LOG2E)
                value_ops.append(operands[4 + 2 * g + half][keys, :])
    maxes = [jnp.maximum(jnp.max(s, axis=-1, keepdims=True), sk) for s, sk in zip(scores, sinks)]
    probs = [jnp.exp2(s - mx).astype(BF16) for s, mx in zip(scores, maxes)]
    sink_p = [jnp.exp2(sk - mx) for sk, mx in zip(sinks, maxes)]
    outs = [_dot(pr, vo) for pr, vo in zip(probs, value_ops)]

    yc_blocks = []
    for j in range(nsub):
        pairs = []
        for p in range(n_pairs):
            i0 = (j * n_pairs + p) * 2
            o_lo, o_hi = outs[i0], outs[i0 + 1]
            num = jnp.where(lo, o_lo, o_hi)
            den = (pltpu.roll(jnp.where(lo, o_hi, o_lo), C_HDIM, axis=1)
                   + jnp.where(lo, sink_p[i0], sink_p[i0 + 1]))
            pairs.append((num * (1.0 / den)).astype(BF16))
        yc_blocks.append(jnp.concatenate(pairs, axis=1))
    yc = jnp.concatenate(yc_blocks, axis=0)

    merged = merged + gt_ref[:, 2 * D_MODEL:3 * D_MODEL].astype(F32) * _dot(yc, wbr_ref[2])
    o_ref[...] = x_ref[...] + _dot(merged.astype(BF16), wout_ref[...])


def _post(o_f, o_b, sg, yb, cq, kv, gt, x, sink, a_norm, wbr, wout, seq, tm):
    m = x.shape[0]
    sub = tm // C_BLOCK
    nblk = m // C_BLOCK
    tiles_per_seq = seq // tm
    row = lambda i: (i, 0)
    return pl.pallas_call(
        functools.partial(_post_kernel, tiles_per_seq=tiles_per_seq),
        out_shape=jax.ShapeDtypeStruct((m, D_MODEL), F32),
        grid=(m // tm,),
        in_specs=[
            pl.BlockSpec(memory_space=pltpu.SMEM),
            pl.BlockSpec((tm, A_W), row), pl.BlockSpec((tm, A_W), row), pl.BlockSpec((tm, A_W), row),
            pl.BlockSpec((tm, BRANCH_W), row), pl.BlockSpec((tm, BRANCH_W), row),
            pl.BlockSpec((tm, KV_W), row),
            pl.BlockSpec((C_BLOCK, KV_W), lambda i: (jnp.maximum(i * sub - 1, 0), 0)),
            pl.BlockSpec((C_BLOCK, KV_W), lambda i: (jnp.minimum((i + 1) * sub, nblk - 1), 0)),
            pl.BlockSpec((tm, N_BRANCH * D_MODEL), row),
            pl.BlockSpec((tm, D_MODEL), row),
            pl.BlockSpec((1, A_DIM), lambda i: (0, 0)),
            pl.BlockSpec((N_BRANCH, BRANCH_W, D_MODEL), lambda i: (0, 0, 0)),
            pl.BlockSpec((D_MODEL, D_MODEL), lambda i: (0, 0)),
        ],
        out_specs=pl.BlockSpec((tm, D_MODEL), row),
        compiler_params=pltpu.CompilerParams(
            dimension_semantics=("arbitrary",), vmem_limit_bytes=VMEM_LIMIT),
        name="post",
    )(sink, o_f, o_b, sg, yb, cq, kv, kv, kv, gt, x, a_norm, wbr, wout)


def _mlp_kernel(x_ref, g_ref, wu_ref, wd_ref, fn_ref, o_ref, *, final):
    x = x_ref[...]
    hn = _rms(x, g_ref[...]).astype(BF16)
    h = jnp.maximum(_dot(hn, wu_ref[...]), 0.0)
    x = x + _dot((h * h).astype(BF16), wd_ref[...])
    if final:
        x = _rms(x, fn_ref[...])
    o_ref[...] = x


def _mlp(x, g, wu, wd, fn, final, tm):
    m = x.shape[0]
    const = lambda r: (0, 0)
    return pl.pallas_call(
        functools.partial(_mlp_kernel, final=final),
        out_shape=jax.ShapeDtypeStruct((m, D_MODEL), F32),
        grid=(m // tm,),
        in_specs=[
            pl.BlockSpec((tm, D_MODEL), lambda r: (r, 0)),
            pl.BlockSpec((1, D_MODEL), const),
            pl.BlockSpec((D_MODEL, D_FF), const, pipeline_mode=pl.Buffered(1)),
            pl.BlockSpec((D_FF, D_MODEL), const, pipeline_mode=pl.Buffered(1)),
            pl.BlockSpec((1, D_MODEL), const),
        ],
        out_specs=pl.BlockSpec((tm, D_MODEL), lambda r: (r, 0)),
        compiler_params=pltpu.CompilerParams(
            dimension_semantics=("arbitrary",), vmem_limit_bytes=VMEM_LIMIT),
        name="mlp",
    )(x, g, wu, wd, fn)


def _rope_table(seq):
    half = ROPE_DIM // 2
    inv = ROPE_THETA ** (-jnp.arange(half, dtype=F32) * (2.0 / ROPE_DIM))
    ang = jnp.arange(seq, dtype=F32)[:, None] * inv[None, :]
    cos, sin = jnp.cos(ang), jnp.sin(ang)
    pad = jnp.zeros((seq, C_HDIM - ROPE_DIM), F32)
    zero = jnp.zeros_like(sin)
    cos_h = jnp.concatenate([cos, cos, pad + 1.0], axis=1)
    sin_a = jnp.concatenate([-sin, zero, pad], axis=1)
    sin_b = jnp.concatenate([zero, sin, pad], axis=1)
    reps = LANES // C_HDIM
    return jnp.concatenate([jnp.tile(t, (1, reps)) for t in (cos_h, sin_a, sin_b)], axis=1)


def kernel(x, w_in, ln1, lb_logits, a_norm, w_s, b_s, sink, w_br, w_out, ln2, w_up, w_down, final_norm):
    bsz, seq, _ = x.shape
    depth = w_in.shape[0]
    m = bsz * seq
    tm = min(512, seq)

    p = jax.nn.softmax(lb_logits.astype(F32), axis=0)
    cum = jnp.cumsum(p, axis=0)
    lower = (cum - cum[0:1]).reshape(depth, 2, 1, A_W)
    gate_consts = jnp.concatenate(
        [lower, 1.0 - lower, jnp.log2(1.0 - lower) - 2.0, jnp.zeros((depth, 2, 5, A_W), F32)], axis=2)

    rope_tab = _rope_table(seq)

    h = x.reshape(m, D_MODEL)
    for l in range(depth):
        bs_tile = jnp.repeat(b_s[l].T, B_CHUNK, axis=1)
        q, v, lf_f, lf_b, k_f, k_b, sg, yb, cq, kv, gt = _inproj(
            h, ln1[l][None, :], w_in[l].astype(BF16), gate_consts[l], rope_tab, w_s[l].astype(BF16), bs_tile,
            seq, tm)
        o_f, o_b = _scan(q, v, lf_f, lf_b, k_f, k_b, bsz, seq)
        h = _post(o_f, o_b, sg, yb, cq, kv, gt, h, sink[l], a_norm[l][None, :],
                  w_br[l].astype(BF16), w_out[l].astype(BF16), seq, tm)
        h = _mlp(h, ln2[l][None, :], w_up[l].astype(BF16), w_down[l].astype(BF16),
                 final_norm[None, :], l == depth - 1, tm)
    return h.reshape(bsz, seq, D_MODEL)
```
